```python
import math
import jax, jax.numpy as jnp
from jax import lax
import numpy as np

D_MODEL = 1024
BATCH = 8
SEQ = 2048
DEPTH = 4
DEC_BATCH = 128
DEC_SEQ = 8
PAST_LEN = 2048
PAGE_SIZE = 128

HEAD_DIM = 64
H_A = 4
DA = 2 * HEAD_DIM
W_A = H_A * DA
H_B = 8
KV_B = 4
W_B = H_B * HEAD_DIM
FORGET_BIAS = 2.0
H_C = 8
G_C = 2
W_C = H_C * HEAD_DIM
CMP_STRIDE = 16
CMP_BLOCK = 2 * CMP_STRIDE
SEL_BLOCK = 64
N_SEL = 16
WINDOW = 512
D_FF = 4 * D_MODEL
QBLK = 128
RMS_EPS = 1e-6
NEG_INF = -1e30
FORCE_BONUS = 1e4

IN_SIZES = (W_A, W_A, W_A, W_B, KV_B * HEAD_DIM, KV_B * HEAD_DIM, H_B, W_C, 6 * G_C * HEAD_DIM, 3 * H_C, 3 * D_MODEL)
N_IN = 3 * W_A + W_B + 2 * KV_B * HEAD_DIM + H_B + W_C + 6 * G_C * HEAD_DIM + 3 * H_C + 3 * D_MODEL

kernel_name = 'hybrid_diff_fox_nsa_decoder_step'


def _rmsnorm(x, g):
    xf = x.astype(jnp.float32)
    y = xf * lax.rsqrt(jnp.mean(xf * xf, axis=-1, keepdims=True) + RMS_EPS)
    return (y * g.astype(jnp.float32)).astype(x.dtype)


def _alibi_slopes(n_heads):
    return 2.0 ** (-8.0 * jnp.arange(1, n_heads + 1, dtype=jnp.float32) / n_heads)


def _masked_softmax(s, mask):
    p = jax.nn.softmax(jnp.where(mask, s, NEG_INF), axis=-1)
    return jnp.where(mask, p, 0.0)


def _mlp(h, w_up, w_down):
    u = jnp.einsum('btd,df->btf', h, w_up)
    return jnp.einsum('btf,fd->btd', jnp.square(jax.nn.relu(u)), w_down)


def _in_proj(h, w_in, b_f):
    b, t, _ = h.shape
    z = jnp.einsum('btd,dc->btc', h, w_in)
    parts, off = [], 0
    for n in IN_SIZES:
        parts.append(z[..., off:off + n])
        off += n
    qa, ka, va, qb, kb, vb, fb, qc, kvc, gc, gm = parts
    logf = jax.nn.log_sigmoid((fb + b_f).astype(jnp.float32))
    return (qa.reshape(b, t, H_A, DA), ka.reshape(b, t, H_A, DA), va.reshape(b, t, H_A, DA),
            qb.reshape(b, t, H_B, HEAD_DIM), kb.reshape(b, t, KV_B, HEAD_DIM), vb.reshape(b, t, KV_B, HEAD_DIM),
            logf, qc.reshape(b, t, H_C, HEAD_DIM), kvc.reshape(b, t, 6, G_C, HEAD_DIM),
            gc.reshape(b, t, 3, H_C), gm.reshape(b, t, 3, D_MODEL))


def _diff_lambda(lam_vec, lam_init):
    lv = lam_vec.astype(jnp.float32)
    return jnp.exp(jnp.sum(lv[0] * lv[1])) - jnp.exp(jnp.sum(lv[2] * lv[3])) + lam_init


def _diff_attend(q, k, v, tq, tk, lam):
    b, nq = q.shape[:2]
    qm = q.reshape(b, nq, H_A, 2, HEAD_DIM)
    km = k.reshape(b, k.shape[1], H_A, 2, HEAD_DIM)
    s = jnp.einsum('bqhmd,bkhmd->bhmqk', qm, km).astype(jnp.float32) * HEAD_DIM ** -0.5
    dist = (tq[:, None] - tk[None, :]).astype(jnp.float32)
    s = s - _alibi_slopes(H_A)[None, :, None, None, None] * dist
    p = _masked_softmax(s, tk[None, :] <= tq[:, None])
    a = p[:, :, 0] - lam * p[:, :, 1]
    return jnp.einsum('bhqk,bkhe->bqhe', a.astype(v.dtype), v)


def _fox_attend(q, k, v, dq, dk, tq, tk):
    b, nq = q.shape[:2]
    nk = k.shape[1]
    r = H_B // KV_B
    qg = q.reshape(b, nq, KV_B, r, HEAD_DIM)
    s = jnp.einsum('bqgrd,bkgd->bgrqk', qg, k).astype(jnp.float32) * HEAD_DIM ** -0.5
    dq_h = dq.reshape(b, nq, KV_B, r).transpose(0, 2, 3, 1)[..., :, None]
    dk_h = dk.reshape(b, nk, KV_B, r).transpose(0, 2, 3, 1)[..., None, :]
    p = _masked_softmax(s + dq_h - dk_h, tk[None, :] <= tq[:, None])
    o = jnp.einsum('bgrqk,bkgd->bqgrd', p.astype(v.dtype), v)
    return o.reshape(b, nq, W_B)


def _nsa_compress(kc, vc):
    b, length = kc.shape[:2]

    def pool(x):
        m = x.reshape(b, length // CMP_STRIDE, CMP_STRIDE, G_C, HEAD_DIM).astype(jnp.float32).mean(2)
        return (0.5 * (m[:, :-1] + m[:, 1:])).astype(x.dtype)

    cpos = jnp.arange(length // CMP_STRIDE - 1) * CMP_STRIDE + (CMP_BLOCK - 1)
    return pool(kc), pool(vc), cpos


def _nsa_attend(q, tq, ck, cv, cpos, sk, sv, wk, wv, wpos):
    b, nq = q.shape[:2]
    r = H_C // G_C
    scale = HEAD_DIM ** -0.5
    slopes = _alibi_slopes(H_C).reshape(G_C, r)[None, :, :, None, None]
    qg = q.reshape(b, nq, G_C, r, HEAD_DIM)
    tqf = tq.astype(jnp.float32)
    s = jnp.einsum('bqgrd,bcgd->bgrqc', qg, ck).astype(jnp.float32) * scale
    s = s - slopes * (tqf[:, None] - cpos[None, :].astype(jnp.float32))
    p_cmp = _masked_softmax(s, cpos[None, :] <= tq[:, None])
    o_cmp = jnp.einsum('bgrqc,bcgd->bqgrd', p_cmp.astype(cv.dtype), cv)
    nb = sk.shape[1] // SEL_BLOCK
    imp = jnp.pad(p_cmp.sum(2), ((0, 0), (0, 0), (0, 0), (0, 1)))
    imp = imp.reshape(b, G_C, nq, nb, SEL_BLOCK // CMP_STRIDE).sum(-1)
    jb = jnp.arange(nb)[None, :]
    cur = (tq // SEL_BLOCK)[:, None]
    valid = jb * SEL_BLOCK <= tq[:, None]
    forced = (jb == 0) | (jb == cur) | (jb == cur - 1)
    score = jnp.where(valid, imp + FORCE_BONUS * forced.astype(jnp.float32), NEG_INF)
    n_sel = min(N_SEL, nb)
    top_val, top_idx = lax.top_k(score, n_sel)
    sel_ok = top_val > 0.5 * NEG_INF

    def gather(x):
        blocks = x.reshape(b, nb, SEL_BLOCK, G_C, HEAD_DIM).transpose(0, 3, 1, 2, 4)
        return jax.vmap(jax.vmap(lambda blk, ix: blk[ix]))(blocks, top_idx)

    gk, gv = gather(sk), gather(sv)
    kpos = top_idx[..., None] * SEL_BLOCK + jnp.arange(SEL_BLOCK)
    smask = sel_ok[..., None] & (kpos <= tq[:, None, None])
    s = jnp.einsum('bqgrd,bgqnkd->bgrqnk', qg, gk).astype(jnp.float32) * scale
    s = s - slopes[..., None] * (tqf[:, None, None] - kpos[:, :, None].astype(jnp.float32))
    m_sel = n_sel * SEL_BLOCK
    p = _masked_softmax(s.reshape(b, G_C, r, nq, m_sel), smask[:, :, None].reshape(b, G_C, 1, nq, m_sel))
    o_sel = jnp.einsum('bgrqm,bgqmd->bqgrd', p.astype(sv.dtype), gv.reshape(b, G_C, nq, m_sel, HEAD_DIM))
    s = jnp.einsum('bqgrd,bsgd->bgrqs', qg, wk).astype(jnp.float32) * scale
    s = s - slopes * (tqf[:, None] - wpos[None, :].astype(jnp.float32))
    wmask = (wpos[None, :] <= tq[:, None]) & (tq[:, None] - wpos[None, :] < WINDOW) & (wpos[None, :] >= 0)
    p = _masked_softmax(s, wmask)
    o_win = jnp.einsum('bgrqs,bsgd->bqgrd', p.astype(wv.dtype), wv)
    shp = (b, nq, H_C, HEAD_DIM)
    return (o_cmp.reshape(shp), o_sel.reshape(shp), o_win.reshape(shp))


def _sweep(fn, t):
    outs = lax.map(fn, jnp.arange(t // QBLK))
    return jax.tree_util.tree_map(lambda o: jnp.moveaxis(o, 0, 1).reshape((o.shape[1], t) + o.shape[3:]), outs)


def _finish(oa, ob, oc, gc, gm, lam_init, g_head, wb_a, wb_b, wb_c, w_o):
    b, t = oa.shape[:2]
    dt = oa.dtype
    of = oa.astype(jnp.float32)
    of = of * lax.rsqrt(jnp.mean(of * of, axis=-1, keepdims=True) + RMS_EPS) * g_head.astype(jnp.float32) * (1.0 - lam_init)
    oa = of.astype(dt).reshape(b, t, W_A)
    gcs = jax.nn.sigmoid(gc.astype(jnp.float32)).astype(dt)[..., None]
    o_cmp, o_sel, o_win = oc
    oc = (gcs[:, :, 0] * o_cmp + gcs[:, :, 1] * o_sel + gcs[:, :, 2] * o_win).reshape(b, t, W_C)
    gate = jax.nn.sigmoid(gm.astype(jnp.float32)).astype(dt)
    m = (gate[:, :, 0] * jnp.einsum('btc,cd->btd', oa, wb_a)
         + gate[:, :, 1] * jnp.einsum('btc,cd->btd', ob, wb_b)
         + gate[:, :, 2] * jnp.einsum('btc,cd->btd', oc, wb_c))
    return jnp.einsum('btd,de->bte', m, w_o)


def _mixer_prompt(h, lam_init, w_in, b_f, lam_vec, g_head, wb_a, wb_b, wb_c, w_o):
    b, t, _ = h.shape
    qa, ka, va, qb, kb, vb, logf, qc, kvc, gc, gm = _in_proj(h, w_in, b_f)
    lam = _diff_lambda(lam_vec, lam_init)
    tk = jnp.arange(t)

    def qslice(a, i):
        return lax.dynamic_slice_in_dim(a, i * QBLK, QBLK, axis=1)

    def diff_blk(i):
        return _diff_attend(qslice(qa, i), ka, va, i * QBLK + jnp.arange(QBLK), tk, lam)

    oa = _sweep(diff_blk, t)
    dcum = jnp.cumsum(logf, axis=1)

    def fox_blk(i):
        return _fox_attend(qslice(qb, i), kb, vb, qslice(dcum, i), dcum, i * QBLK + jnp.arange(QBLK), tk)

    ob = _sweep(fox_blk, t)
    ck, cv, cpos = _nsa_compress(kvc[:, :, 0], kvc[:, :, 1])
    sk, sv = kvc[:, :, 2], kvc[:, :, 3]
    wpad = jnp.pad(kvc[:, :, 4:6], ((0, 0), (WINDOW, 0), (0, 0), (0, 0), (0, 0)))

    def nsa_blk(i):
        w = lax.dynamic_slice_in_dim(wpad, i * QBLK, QBLK + WINDOW, axis=1)
        wpos = i * QBLK - WINDOW + jnp.arange(QBLK + WINDOW)
        return _nsa_attend(qslice(qc, i), i * QBLK + jnp.arange(QBLK), ck, cv, cpos, sk, sv,
                           w[:, :, 0], w[:, :, 1], wpos)

    oc = _sweep(nsa_blk, t)
    y = _finish(oa, ob, oc, gc, gm, lam_init, g_head, wb_a, wb_b, wb_c, w_o)
    n_win = min(WINDOW, t)
    state = (jnp.stack([ka, va], axis=2), jnp.stack([kb, vb], axis=2), logf,
             kvc[:, :, :4], kvc[:, t - n_win:, 4:6])
    return y, state


def _mixer_sample(h, l, lam_init, cache_diff_kv, cache_fox_kv, cache_fox_logf, cache_nsa_kv,
                  state_nsa_win_kv, page_table, w_in, b_f, lam_vec, g_head, wb_a, wb_b, wb_c, w_o):
    bd, t, _ = h.shape
    qa, ka, va, qb, kb, vb, logf, qc, kvc, gc, gm = _in_proj(h, w_in, b_f)
    lam = _diff_lambda(lam_vec, lam_init)
    tq = PAST_LEN + jnp.arange(t)
    tk = jnp.arange(PAST_LEN + t)

    def paged(cache):
        g = cache[l, page_table]
        return g.reshape((bd, PAST_LEN) + g.shape[3:])

    kvd = paged(cache_diff_kv)
    oa = _diff_attend(qa, jnp.concatenate([kvd[:, :, 0], ka], axis=1),
                      jnp.concatenate([kvd[:, :, 1], va], axis=1), tq, tk, lam)
    kvf = paged(cache_fox_kv)
    d_past = jnp.cumsum(paged(cache_fox_logf).astype(jnp.float32), axis=1)
    d_new = d_past[:, -1:] + jnp.cumsum(logf, axis=1)
    ob = _fox_attend(qb, jnp.concatenate([kvf[:, :, 0], kb], axis=1), jnp.concatenate([kvf[:, :, 1], vb], axis=1),
                     d_new, jnp.concatenate([d_past, d_new], axis=1), tq, tk)
    pad_new = (-t) % SEL_BLOCK
    full = jnp.concatenate([paged(cache_nsa_kv),
                            jnp.pad(kvc[:, :, :4], ((0, 0), (0, pad_new), (0, 0), (0, 0), (0, 0)))], axis=1)
    ck, cv, cpos = _nsa_compress(full[:, :, 0], full[:, :, 1])
    win = state_nsa_win_kv[l]
    wbuf = win.shape[1]
    wkv = jnp.concatenate([win, kvc[:, :, 4:6]], axis=1)
    wpos = PAST_LEN - wbuf + jnp.arange(wbuf + t)
    oc = _nsa_attend(qc, tq, ck, cv, cpos, full[:, :, 2], full[:, :, 3], wkv[:, :, 0], wkv[:, :, 1], wpos)
    y = _finish(oa, ob, oc, gc, gm, lam_init, g_head, wb_a, wb_b, wb_c, w_o)
    state = (jnp.stack([ka, va], axis=2), jnp.stack([kb, vb], axis=2), logf,
             kvc[:, :, :4], kvc[:, :, 4:6])
    return y, state


def setup_inputs(seed: int = 0) -> dict:
    key = jax.random.key(seed)
    ks = jax.random.split(key, 20)
    n_pages = PAST_LEN // PAGE_SIZE
    n_used = DEC_BATCH * n_pages
    n_phys = n_used + n_used // 4
    win_buf = min(WINDOW, PAST_LEN)

    def nrm(k, shape, scale=1.0):
        return scale * jax.random.normal(k, shape, jnp.float32)

    return {
        'x_prompt': nrm(ks[0], (BATCH, SEQ, D_MODEL)),
        'x_sample': nrm(ks[1], (DEC_BATCH, DEC_SEQ, D_MODEL)),
        'cache_diff_kv': nrm(ks[2], (DEPTH, n_phys, PAGE_SIZE, 2, H_A, DA)),
        'cache_fox_kv': nrm(ks[3], (DEPTH, n_phys, PAGE_SIZE, 2, KV_B, HEAD_DIM)),
        'cache_fox_logf': jax.nn.log_sigmoid(FORGET_BIAS + nrm(ks[4], (DEPTH, n_phys, PAGE_SIZE, H_B))),
        'cache_nsa_kv': nrm(ks[5], (DEPTH, n_phys, PAGE_SIZE, 4, G_C, HEAD_DIM)),
        'state_nsa_win_kv': nrm(ks[6], (DEPTH, DEC_BATCH, win_buf, 2, G_C, HEAD_DIM)),
        'page_table': jax.random.permutation(ks[7], n_phys)[:n_used].reshape(DEC_BATCH, n_pages).astype(jnp.int32),
        'w_in': nrm(ks[8], (DEPTH, D_MODEL, N_IN), D_MODEL ** -0.5),
        'b_f': FORGET_BIAS + nrm(ks[9], (DEPTH, H_B), 0.1),
        'diff_lam': nrm(ks[10], (DEPTH, 4, HEAD_DIM), 0.1),
        'diff_norm_g': 1.0 + nrm(ks[11], (DEPTH, H_A, DA), 0.05),
        'w_branch_a': nrm(ks[12], (DEPTH, W_A, D_MODEL), W_A ** -0.5),
        'w_branch_b': nrm(ks[13], (DEPTH, W_B, D_MODEL), W_B ** -0.5),
        'w_branch_c': nrm(ks[14], (DEPTH, W_C, D_MODEL), W_C ** -0.5),
        'w_out': nrm(ks[15], (DEPTH, D_MODEL, D_MODEL), D_MODEL ** -0.5),
        'norm_g': 1.0 + nrm(ks[16], (DEPTH, 4, D_MODEL), 0.05),
        'w_up': nrm(ks[17], (DEPTH, D_MODEL, D_FF), D_MODEL ** -0.5),
        'w_down': nrm(ks[18], (DEPTH, D_FF, D_MODEL), D_FF ** -0.5),
    }


def reference(x_prompt, x_sample, cache_diff_kv, cache_fox_kv, cache_fox_logf, cache_nsa_kv,
              state_nsa_win_kv, page_table, w_in, b_f, diff_lam, diff_norm_g, w_branch_a,
              w_branch_b, w_branch_c, w_out, norm_g, w_up, w_down):
    yp, ys = x_prompt, x_sample
    outs_p = ([], [], [], [], [])
    outs_s = ([], [], [], [], [])
    for l in range(DEPTH):
        lam_init = 0.8 - 0.6 * math.exp(-0.3 * l)
        lw = (w_in[l], b_f[l], diff_lam[l], diff_norm_g[l], w_branch_a[l], w_branch_b[l], w_branch_c[l], w_out[l])
        mix, st = _mixer_prompt(_rmsnorm(yp, norm_g[l, 0]), lam_init, *lw)
        yp = yp + _rmsnorm(mix, norm_g[l, 1])
        yp = yp + _rmsnorm(_mlp(_rmsnorm(yp, norm_g[l, 2]), w_up[l], w_down[l]), norm_g[l, 3])
        for lst, s in zip(outs_p, st):
            lst.append(s)
        mix, st = _mixer_sample(_rmsnorm(ys, norm_g[l, 0]), l, lam_init, cache_diff_kv, cache_fox_kv,
                                cache_fox_logf, cache_nsa_kv, state_nsa_win_kv, page_table, *lw)
        ys = ys + _rmsnorm(mix, norm_g[l, 1])
        ys = ys + _rmsnorm(_mlp(_rmsnorm(ys, norm_g[l, 2]), w_up[l], w_down[l]), norm_g[l, 3])
        for lst, s in zip(outs_s, st):
            lst.append(s)
    dkv_p, fkv_p, flf_p, nkv_p, nwin_p = [jnp.stack(v) for v in outs_p]
    dkv_s, fkv_s, flf_s, nkv_s, nwin_s = [jnp.stack(v) for v in outs_s]
    return (yp, ys, dkv_p, dkv_s, fkv_p, fkv_s, flf_p, flf_s, nkv_p, nkv_s, nwin_p, nwin_s)
```

```python
import functools

import numpy as np
import jax
import jax.numpy as jnp
from jax import lax
from jax.experimental import pallas as pl
from jax.experimental.pallas import tpu as pltpu

F32 = jnp.float32
BF16 = jnp.bfloat16

D_MODEL = 1024
HEAD_DIM = 64
H_A = 4
DA = 2 * HEAD_DIM
W_A = H_A * DA
H_B = 8
KV_B = 4
W_B = H_B * HEAD_DIM
H_C = 8
G_C = 2
W_C = H_C * HEAD_DIM
CMP_STRIDE = 16
CMP_BLOCK = 2 * CMP_STRIDE
SEL_BLOCK = 64
N_SEL = 16
WINDOW = 512
D_FF = 4 * D_MODEL
PAGE = 128
RMS_EPS = 1e-6
NEG_INF = -1e30
FORCE_BONUS = 1e4
QK_SCALE = HEAD_DIM ** -0.5

LANES = 128
HALF = LANES // 2
TQ = 256
VMEM_LIMIT = 56 * 1024 * 1024

_OFF_QA, _OFF_KA, _OFF_VA = 0, W_A, 2 * W_A
_OFF_QB = 3 * W_A
_OFF_KB = _OFF_QB + W_B
_OFF_VB = _OFF_KB + KV_B * HEAD_DIM
_OFF_FB = _OFF_VB + KV_B * HEAD_DIM
_OFF_QC = _OFF_FB + H_B
_OFF_KVC = _OFF_QC + W_C
_OFF_GC = _OFF_KVC + 6 * G_C * HEAD_DIM
_OFF_GM = _OFF_GC + 3 * H_C
N_IN = _OFF_GM + 3 * D_MODEL

PERM_B = (0, 2, 1, 3, 4, 6, 5, 7)
PERM_C = (0, 4, 1, 5, 2, 6, 3, 7)

R_DIFF = 0
R_FOX = R_DIFF + 2 * W_A
R_NSA = R_FOX + 2 * KV_B * HEAD_DIM
R_FBGC = R_NSA + 6 * G_C * HEAD_DIM
R_Q = R_FBGC + LANES
R_GM = R_Q + W_A + W_B + W_C
NW = R_GM + 3 * D_MODEL
GC_LANE0 = H_B

NT_DIMS = (((1,), (1,)), ((), ()))


def _dot(a, b):
    return jnp.dot(a, b, preferred_element_type=F32)


def _dot_nt(a, b):
    return lax.dot_general(a, b, NT_DIMS, preferred_element_type=F32)


def _rms(x, g):
    return x * lax.rsqrt(jnp.mean(x * x, axis=-1, keepdims=True) + RMS_EPS) * g


def _log_sigmoid(x):
    return jnp.minimum(x, 0.0) - jnp.log1p(jnp.exp(-jnp.abs(x)))


def _tile_lanes(m, n):
    reps = n // m.shape[1]
    return m if reps == 1 else jnp.concatenate([m] * reps, axis=1)


def _params(*sem):
    return pltpu.CompilerParams(dimension_semantics=sem, vmem_limit_bytes=VMEM_LIMIT)


def _resident(shape, imap):
    return pl.BlockSpec(shape, imap, pipeline_mode=pl.Buffered(1))


def _inproj_prompt_body(x_ref, g_ref, bf_ref, w_ref, dkv_ref, dkvb_ref, foxt_ref, foxtb_ref,
                        nsat_ref, wint_ref, nsatb_ref, logft_ref, gc_ref, q_ref, gate_ref):
    h = _rms(x_ref[...], g_ref[...]).astype(BF16)

    def nn(lo, n):
        return _dot_nt(h, w_ref[lo:lo + n, :])

    def tt(lo, n):
        return _dot_nt(w_ref[lo:lo + n, :], h)

    z = nn(R_DIFF, 2 * W_A)
    dkv_ref[...] = z
    dkvb_ref[...] = z.astype(BF16)
    z = tt(R_FOX, 2 * KV_B * HEAD_DIM)
    foxt_ref[...] = z
    foxtb_ref[...] = z.astype(BF16)
    z = tt(R_NSA, 6 * G_C * HEAD_DIM)
    nsat_ref[...] = z[0:4 * G_C * HEAD_DIM]
    wint_ref[...] = z[4 * G_C * HEAD_DIM:]
    nsatb_ref[...] = z.astype(BF16)
    z = tt(R_FBGC, 16)
    logft_ref[...] = _log_sigmoid(z[0:H_B] + bf_ref[...])
    gc_ref[...] = nn(R_FBGC, LANES)
    q_ref[...] = nn(R_Q, W_A + W_B + W_C).astype(BF16)
    gate_ref[...] = jax.nn.sigmoid(nn(R_GM, 3 * D_MODEL)).astype(BF16)


def _inproj_sample_body(x_ref, g_ref, bf_ref, bfrow_ref, w_ref, dkv_ref, fox_ref, nsa_ref, logft_ref, fbgc_ref,
                        q_ref, gate_ref):
    h = _rms(x_ref[...], g_ref[...]).astype(BF16)

    def nn(lo, n):
        return _dot_nt(h, w_ref[lo:lo + n, :])

    dkv_ref[...] = nn(R_DIFF, 2 * W_A)
    fox_ref[...] = nn(R_FOX, 2 * KV_B * HEAD_DIM)
    nsa_ref[...] = nn(R_NSA, 6 * G_C * HEAD_DIM)
    zt = _dot_nt(w_ref[R_FBGC:R_FBGC + 16, :], h)
    logft_ref[...] = _log_sigmoid(zt[0:H_B] + bf_ref[...])
    z = nn(R_FBGC, LANES)
    lane = lax.broadcasted_iota(jnp.int32, z.shape, 1)
    fbgc_ref[...] = jnp.where(lane < H_B, _log_sigmoid(z + bfrow_ref[...]), z)
    q_ref[...] = nn(R_Q, W_A + W_B + W_C)
    gate_ref[...] = jax.nn.sigmoid(nn(R_GM, 3 * D_MODEL)).astype(BF16)


def _inproj_prompt(x, g, bf_col, wt, n_batch, seq):
    n = x.shape[0]
    nq = seq // TQ
    grid = (n_batch, nq)
    row = lambda b, i: (b * nq + i, 0)
    const2 = lambda b, i: (0, 0)
    tr = lambda b, i: (b, 0, i)
    out_shape = (
        jax.ShapeDtypeStruct((n, 2 * W_A), F32),
        jax.ShapeDtypeStruct((n, 2 * W_A), BF16),
        jax.ShapeDtypeStruct((n_batch, 512, seq), F32),
        jax.ShapeDtypeStruct((n_batch, nq, 512, TQ), BF16),
        jax.ShapeDtypeStruct((n_batch, 512, seq), F32),
        jax.ShapeDtypeStruct((n_batch, 256, seq), F32),
        jax.ShapeDtypeStruct((n_batch, nq, 768, TQ), BF16),
        jax.ShapeDtypeStruct((n_batch, H_B, seq), F32),
        jax.ShapeDtypeStruct((n, LANES), F32),
        jax.ShapeDtypeStruct((n, W_A + W_B + W_C), BF16),
        jax.ShapeDtypeStruct((n, 3 * D_MODEL), BF16),
    )
    out_specs = (
        pl.BlockSpec((TQ, 2 * W_A), row),
        pl.BlockSpec((TQ, 2 * W_A), row),
        pl.BlockSpec((None, 512, TQ), tr),
        pl.BlockSpec((None, None, 512, TQ), lambda b, i: (b, i, 0, 0)),
        pl.BlockSpec((None, 512, TQ), tr),
        pl.BlockSpec((None, 256, TQ), tr),
        pl.BlockSpec((None, None, 768, TQ), lambda b, i: (b, i, 0, 0)),
        pl.BlockSpec((None, H_B, TQ), tr),
        pl.BlockSpec((TQ, LANES), row),
        pl.BlockSpec((TQ, W_A + W_B + W_C), row),
        pl.BlockSpec((TQ, 3 * D_MODEL), row),
    )
    in_specs = [
        pl.BlockSpec((TQ, D_MODEL), row),
        pl.BlockSpec((1, D_MODEL), const2),
        pl.BlockSpec((H_B, 1), const2),
        _resident((NW, D_MODEL), const2),
    ]
    return pl.pallas_call(
        _inproj_prompt_body, grid=grid, in_specs=in_specs, out_specs=out_specs, out_shape=out_shape,
        compiler_params=_params("arbitrary", "arbitrary"), name="inproj_prompt",
    )(x, g, bf_col, wt)


def _inproj_sample(x, g, bf_col, bf_row, wt):
    n = x.shape[0]
    tm = min(TQ, n)
    grid = (n // tm,)
    row = lambda i: (i, 0)
    const2 = lambda i: (0, 0)
    widths = ((2 * W_A, F32), (512, F32), (768, F32))
    out_shape = tuple(jax.ShapeDtypeStruct((n, w), dt) for w, dt in widths) + (
        jax.ShapeDtypeStruct((H_B, n), F32),
        jax.ShapeDtypeStruct((n, LANES), F32),
        jax.ShapeDtypeStruct((n, W_A + W_B + W_C), F32),
        jax.ShapeDtypeStruct((n, 3 * D_MODEL), BF16),
    )
    out_specs = tuple(pl.BlockSpec((tm, w), row) for w, _ in widths) + (
        pl.BlockSpec((H_B, tm), lambda i: (0, i)),
        pl.BlockSpec((tm, LANES), row),
        pl.BlockSpec((tm, W_A + W_B + W_C), row),
        pl.BlockSpec((tm, 3 * D_MODEL), row),
    )
    in_specs = [
        pl.BlockSpec((tm, D_MODEL), row),
        pl.BlockSpec((1, D_MODEL), const2),
        pl.BlockSpec((H_B, 1), const2),
        pl.BlockSpec((1, LANES), const2),
        _resident((NW, D_MODEL), const2),
    ]
    return pl.pallas_call(
        _inproj_sample_body, grid=grid, in_specs=in_specs, out_specs=out_specs, out_shape=out_shape,
        compiler_params=_params("arbitrary"), name="inproj_sample",
    )(x, g, bf_col, bf_row, wt)


def _flash_init(m_scr, l_scr, acc_scr):
    m_scr[...] = jnp.full(m_scr.shape, NEG_INF, F32)
    l_scr[...] = jnp.zeros(l_scr.shape, F32)
    acc_scr[...] = jnp.zeros(acc_scr.shape, F32)


def _flash_update(s, idx, m_scr, l_scr, acc_scr, pv):
    m_prev = m_scr[idx]
    m_next = jnp.maximum(m_prev, jnp.max(s, axis=1, keepdims=True))
    alpha = jnp.exp(m_prev - m_next)
    p = jnp.exp(s - _tile_lanes(m_next, s.shape[1]))
    l_scr[idx] = alpha * l_scr[idx] + jnp.sum(p, axis=1, keepdims=True)
    m_scr[idx] = m_next
    acc_scr[idx] = alpha * acc_scr[idx] + pv(p.astype(BF16))


def _causal_bias(n):
    r = lax.broadcasted_iota(jnp.int32, (n, n), 0)
    c = lax.broadcasted_iota(jnp.int32, (n, n), 1)
    return jnp.where(c <= r, 0.0, NEG_INF).astype(F32)


def _half_masks(shape):
    lane = lax.broadcasted_iota(jnp.int32, shape, 1)
    return lane < HALF, lane >= HALF


def _scaled_halves(q_tile):
    q = q_tile.astype(F32) * QK_SCALE
    lo, hi = _half_masks(q.shape)
    return jnp.where(lo, q, 0.0).astype(BF16), jnp.where(hi, q, 0.0).astype(BF16)


def _diff_lambda(lam_ref, lam_init):
    lv = lam_ref[...]
    a = jnp.sum(lv[0:1] * lv[1:2], axis=1, keepdims=True)
    b = jnp.sum(lv[2:3] * lv[3:4], axis=1, keepdims=True)
    return jnp.exp(a) - jnp.exp(b) + lam_init


def _head_norm(o, g_row, lam_init):
    return o * lax.rsqrt(jnp.mean(o * o, axis=-1, keepdims=True) + RMS_EPS) * g_row * (1.0 - lam_init)


def _alibi_slope(h, n_heads):
    return float(2.0 ** (-8.0 * (h + 1) / n_heads))


def _diff_prompt_body(lam_ref, gh_ref, q_ref, k_ref, v_ref, o_ref, m_scr, l_scr, acc_scr, *, lam_init):
    i = pl.program_id(1)
    tq = q_ref.shape[0]
    lam = _diff_lambda(lam_ref, lam_init)
    tri = _causal_bias(tq)
    kiota = lax.broadcasted_iota(jnp.int32, (1, tq), 1)
    for h in range(H_A):
        cols = slice(h * DA, (h + 1) * DA)
        q1, q2 = _scaled_halves(q_ref[:, cols])
        slope = _alibi_slope(h, H_A)
        _flash_init(m_scr, l_scr, acc_scr)

        def step(j, extra, q1=q1, q2=q2, slope=slope, cols=cols):
            rows = pl.ds(pl.multiple_of(j * tq, tq), tq)
            kc = k_ref[rows, cols]
            vc = v_ref[rows, cols]
            bias = slope * ((j - i) * tq + kiota).astype(F32)
            if extra is not None:
                bias = bias + extra
            for mi, qm in enumerate((q1, q2)):
                _flash_update(_dot_nt(qm, kc) + bias, mi, m_scr, l_scr, acc_scr, lambda p: _dot(p, vc))

        def body(j, c):
            step(j, None)
            return c

        lax.fori_loop(0, i, body, 0)
        step(i, tri)
        o = acc_scr[0] / l_scr[0] - lam * (acc_scr[1] / l_scr[1])
        o_ref[:, cols] = _head_norm(o, gh_ref[h:h + 1, :], lam_init).astype(BF16)


def _diff_prompt(lam_vec, g_head, q_all, kv_b, n_batch, seq, lam_init):
    nq = seq // TQ
    n = q_all.shape[0]
    return pl.pallas_call(
        functools.partial(_diff_prompt_body, lam_init=lam_init),
        grid=(n_batch, nq),
        in_specs=[
            pl.BlockSpec((4, HEAD_DIM), lambda b, i: (0, 0)),
            pl.BlockSpec((H_A, DA), lambda b, i: (0, 0)),
            pl.BlockSpec((TQ, W_A), lambda b, i: (b * nq + i, 0)),
            pl.BlockSpec((seq, W_A), lambda b, i: (b, 0)),
            pl.BlockSpec((seq, W_A), lambda b, i: (b, 1)),
        ],
        out_specs=pl.BlockSpec((TQ, W_A), lambda b, i: (b * nq + i, 0)),
        out_shape=jax.ShapeDtypeStruct((n, W_A), BF16),
        scratch_shapes=[pltpu.VMEM((2, TQ, LANES), F32)] * 3,
        compiler_params=_params("arbitrary", "arbitrary"), name="diff_prompt",
    )(lam_vec, g_head, q_all, kv_b, kv_b)


def _lane_cumsum(x):
    n = x.shape[1]
    lane = lax.broadcasted_iota(jnp.int32, x.shape, 1)
    sh = 1
    while sh < n:
        x = x + jnp.where(lane >= sh, pltpu.roll(x, sh, axis=1), 0.0)
        sh *= 2
    return x


def _pair_heads_b(gp):
    return ((2 * gp, 0, 4 * gp), (2 * gp, 1, 4 * gp + 2), (2 * gp + 1, 0, 4 * gp + 1), (2 * gp + 1, 1, 4 * gp + 3))


def _fox_prompt_body(q_ref, kvt_ref, lf_ref, o_ref, d_scr, m_scr, l_scr, acc_scr):
    i = pl.program_id(1)
    tq = q_ref.shape[0]
    nchunk = kvt_ref.shape[0]

    @pl.when(i == 0)
    def _():
        d = _lane_cumsum(lf_ref[...])
        for c in range(nchunk):
            d_scr[c] = d[:, c * tq:(c + 1) * tq]

    tri = _causal_bias(tq)
    lo, _ = _half_masks((tq, LANES))
    for gp in range(KV_B // 2):
        heads = _pair_heads_b(gp)
        halves = [_scaled_halves(q_ref[:, t * LANES:(t + 1) * LANES]) for t in (2 * gp, 2 * gp + 1)]
        qms = [halves[t - 2 * gp][half] for t, half, _ in heads]
        _flash_init(m_scr, l_scr, acc_scr)
        krows = slice(gp * LANES, (gp + 1) * LANES)
        vrows = slice(KV_B * HEAD_DIM + gp * LANES, KV_B * HEAD_DIM + (gp + 1) * LANES)

        def step(j, extra, qms=qms, heads=heads, krows=krows, vrows=vrows):
            ktc = kvt_ref[j, krows, :]
            vtc = kvt_ref[j, vrows, :]
            dj = d_scr[j]
            for idx, (_, _, h) in enumerate(heads):
                s = _dot(qms[idx], ktc) - dj[h:h + 1, :]
                if extra is not None:
                    s = s + extra
                _flash_update(s, idx, m_scr, l_scr, acc_scr, lambda p: _dot_nt(p, vtc))

        def body(j, c):
            step(j, None)
            return c

        lax.fori_loop(0, i, body, 0)
        step(i, tri)
        outs = [acc_scr[idx] / l_scr[idx] for idx in range(4)]
        o_ref[:, (2 * gp) * LANES:(2 * gp + 1) * LANES] = jnp.where(lo, outs[0], outs[1]).astype(BF16)
        o_ref[:, (2 * gp + 1) * LANES:(2 * gp + 2) * LANES] = jnp.where(lo, outs[2], outs[3]).astype(BF16)


def _fox_prompt(q_all, foxtb, logft, n_batch, seq):
    nq = seq // TQ
    n = q_all.shape[0]
    return pl.pallas_call(
        _fox_prompt_body,
        grid=(n_batch, nq),
        in_specs=[
            pl.BlockSpec((TQ, W_B), lambda b, i: (b * nq + i, 1)),
            pl.BlockSpec((None, nq, 512, TQ), lambda b, i: (b, 0, 0, 0)),
            pl.BlockSpec((None, H_B, seq), lambda b, i: (b, 0, 0)),
        ],
        out_specs=pl.BlockSpec((TQ, W_B), lambda b, i: (b * nq + i, 0)),
        out_shape=jax.ShapeDtypeStruct((n, W_B), BF16),
        scratch_shapes=[pltpu.VMEM((nq, H_B, TQ), F32)] + [pltpu.VMEM((4, TQ, LANES), F32)] * 3,
        compiler_params=_params("arbitrary", "arbitrary"), name="fox_prompt",
    )(q_all, foxtb, logft)


def _block_scores(imp, qpos, n_blocks):
    lane = lax.broadcasted_iota(jnp.int32, imp.shape, 1)
    y = imp + pltpu.roll(imp, 1, axis=1)
    bs = y + pltpu.roll(y, 2, axis=1)
    blk = lane >> 2
    valid = ((lane & 3) == 3) & (blk < n_blocks) & (blk * SEL_BLOCK <= qpos)
    cur = qpos >> 6
    forced = (blk == 0) | (blk == cur) | (blk == cur - 1)
    score = jnp.where(valid, bs + FORCE_BONUS * jnp.where(forced, 1.0, 0.0), NEG_INF)
    return score, valid, blk


def _select_blocks(score, valid, blk, n_blocks):
    cnt = jnp.zeros(score.shape, F32)
    for jp in range(n_blocks):
        v = score[:, 4 * jp + 3:4 * jp + 4]
        better = (v > score) | ((v == score) & (blk > jp))
        cnt = cnt + jnp.where(better, 1.0, 0.0)
    return valid & (cnt < float(min(N_SEL, n_blocks)))


def _nsa_prompt_body(q_ref, kvt_ref, gc_ref, pool_ref, e_ref, o_ref, ck_scr, cv_scr, selb_scr, winb_scr,
                     m_scr, l_scr, acc_scr, oc_scr):
    i = pl.program_id(1)
    tq = q_ref.shape[0]
    nchunk = kvt_ref.shape[0]
    n_blocks = nchunk * tq // SEL_BLOCK
    q0 = i * tq

    @pl.when(i == 0)
    def _():
        ck = jnp.zeros((LANES, LANES), F32)
        cv = jnp.zeros((LANES, LANES), F32)
        for c in range(nchunk):
            ck = ck + _dot(kvt_ref[c, 0:LANES, :], pool_ref[c])
            cv = cv + _dot(kvt_ref[c, LANES:2 * LANES, :], pool_ref[c])
        ck_scr[...] = ck.astype(BF16)
        cv_scr[...] = cv.astype(BF16)

    row = lax.broadcasted_iota(jnp.int32, (tq, LANES), 0)
    lane = lax.broadcasted_iota(jnp.int32, (tq, LANES), 1)
    qpos = q0 + row
    cpos = lane * CMP_STRIDE + (CMP_BLOCK - 1)
    cmp_ok = cpos <= qpos
    cmp_bias = jnp.where(cmp_ok, 0.0, NEG_INF)
    cpos_rel = (cpos - q0).astype(F32)

    row_k = lax.broadcasted_iota(jnp.int32, (tq, tq), 0) + q0
    col_k = lax.broadcasted_iota(jnp.int32, (tq, tq), 1)
    kiota = lax.broadcasted_iota(jnp.int32, (1, tq), 1)
    n_win = WINDOW // tq + 1
    for slot in range(n_win):
        kpos = (i - (n_win - 1) + slot) * tq + col_k
        ok = (kpos <= row_k) & (row_k - kpos < WINDOW) & (kpos >= 0)
        winb_scr[slot] = jnp.where(ok, 0.0, NEG_INF)

    def gate(br, h):
        c = GC_LANE0 + br * H_C + h
        return jax.nn.sigmoid(gc_ref[:, c:c + 1])

    lo, hi = _half_masks((tq, LANES))
    for g in range(G_C):
        heads = [(H_C // G_C) * g + r for r in range(H_C // G_C)]
        qms = [_scaled_halves(q_ref[:, r * LANES:(r + 1) * LANES])[g] for r in range(len(heads))]
        slopes = [_alibi_slope(h, H_C) for h in heads]

        imp = jnp.zeros((tq, LANES), F32)
        ckt = ck_scr[...]
        cvt = cv_scr[...]
        for r, h in enumerate(heads):
            s = _dot(qms[r], ckt) + slopes[r] * cpos_rel + cmp_bias
            p = jnp.exp(s - jnp.max(s, axis=1, keepdims=True))
            p = jnp.where(cmp_ok, p / jnp.sum(p, axis=1, keepdims=True), 0.0)
            imp = imp + p
            oc_scr[h] = gate(0, h) * _dot_nt(p.astype(BF16), cvt)

        score, valid, blk = _block_scores(imp, qpos, n_blocks)
        sel = jnp.where(_select_blocks(score, valid, blk, n_blocks), 1.0, 0.0).astype(BF16)

        def mk_bias(c, carry, sel=sel):
            ex = _dot(sel, e_ref[c])
            ok = (ex > 0.5) & (c * tq + col_k <= row_k)
            selb_scr[c] = jnp.where(ok, 0.0, NEG_INF)
            return carry

        lax.fori_loop(0, i + 1, mk_bias, 0)

        _flash_init(m_scr, l_scr, acc_scr)

        def sel_step(c, carry, qms=qms, slopes=slopes):
            ktc = kvt_ref[c, 2 * LANES:3 * LANES, :]
            vtc = kvt_ref[c, 3 * LANES:4 * LANES, :]
            kpos_rel = ((c - i) * tq + kiota).astype(F32)
            mb = selb_scr[c]
            for r in range(len(qms)):
                s = _dot(qms[r], ktc) + slopes[r] * kpos_rel + mb
                _flash_update(s, r, m_scr, l_scr, acc_scr, lambda p: _dot_nt(p, vtc))
            return carry

        lax.fori_loop(0, i + 1, sel_step, 0)
        for r, h in enumerate(heads):
            oc_scr[h] = oc_scr[h] + gate(1, h) * (acc_scr[r] / l_scr[r])

        _flash_init(m_scr, l_scr, acc_scr)

        def win_step(slot, carry, qms=qms, slopes=slopes):
            c = i - (n_win - 1) + slot
            ktc = kvt_ref[c, 4 * LANES:5 * LANES, :]
            vtc = kvt_ref[c, 5 * LANES:6 * LANES, :]
            kpos_rel = ((c - i) * tq + kiota).astype(F32)
            mb = winb_scr[slot]
            for r in range(len(qms)):
                s = _dot(qms[r], ktc) + slopes[r] * kpos_rel + mb
                _flash_update(s, r, m_scr, l_scr, acc_scr, lambda p: _dot_nt(p, vtc))
            return carry

        lax.fori_loop(jnp.maximum(n_win - 1 - i, 0), n_win, win_step, 0)
        for r, h in enumerate(heads):
            oc_scr[h] = oc_scr[h] + gate(2, h) * (acc_scr[r] / l_scr[r])

    for t in range(H_C // G_C):
        o_ref[:, t * LANES:(t + 1) * LANES] = jnp.where(lo, oc_scr[t], oc_scr[t + H_C // G_C]).astype(BF16)


def _pool_matrix(length, n_cols):
    t = np.arange(length)[:, None]
    c = np.arange(n_cols)[None, :]
    m = (t >= c * CMP_STRIDE) & (t < c * CMP_STRIDE + CMP_BLOCK)
    return (m.astype(np.float32) / CMP_BLOCK)


def _expand_matrix(n_lanes, length):
    r = np.arange(n_lanes)[:, None]
    k = np.arange(length)[None, :]
    return (((r & 3) == 3) & ((r >> 2) == k // SEL_BLOCK)).astype(np.float32)


def _nsa_prompt(q_all, nsatb, gc, n_batch, seq):
    nq = seq // TQ
    n = q_all.shape[0]
    pool = jnp.asarray(_pool_matrix(seq, LANES).reshape(nq, TQ, LANES), BF16)
    expand = jnp.asarray(_expand_matrix(LANES, seq).reshape(LANES, nq, TQ).transpose(1, 0, 2), BF16)
    n_win = WINDOW // TQ + 1
    return pl.pallas_call(
        _nsa_prompt_body,
        grid=(n_batch, nq),
        in_specs=[
            pl.BlockSpec((TQ, W_C), lambda b, i: (b * nq + i, 2)),
            pl.BlockSpec((None, nq, 768, TQ), lambda b, i: (b, 0, 0, 0)),
            pl.BlockSpec((TQ, LANES), lambda b, i: (b * nq + i, 0)),
            pl.BlockSpec((nq, TQ, LANES), lambda b, i: (0, 0, 0)),
            pl.BlockSpec((nq, LANES, TQ), lambda b, i: (0, 0, 0)),
        ],
        out_specs=pl.BlockSpec((TQ, W_C), lambda b, i: (b * nq + i, 0)),
        out_shape=jax.ShapeDtypeStruct((n, W_C), BF16),
        scratch_shapes=[
            pltpu.VMEM((LANES, LANES), BF16), pltpu.VMEM((LANES, LANES), BF16),
            pltpu.VMEM((nq, TQ, TQ), F32), pltpu.VMEM((n_win, TQ, TQ), F32),
            pltpu.VMEM((H_C // G_C, TQ, LANES), F32), pltpu.VMEM((H_C // G_C, TQ, LANES), F32),
            pltpu.VMEM((H_C // G_C, TQ, LANES), F32), pltpu.VMEM((H_C, TQ, LANES), F32),
        ],
        compiler_params=_params("arbitrary", "arbitrary"), name="nsa_prompt",
    )(q_all, nsatb, gc, pool, expand)


def _pad_rows(x, n):
    return jnp.concatenate([x, jnp.zeros((n - x.shape[0], x.shape[1]), x.dtype)], axis=0).astype(BF16)


def _query_index(shape, t):
    assert t & (t - 1) == 0
    return lax.broadcasted_iota(jnp.int32, shape, 0) & (t - 1)


def _two_piece_softmax(s_past, s_new):
    m = jnp.maximum(jnp.max(s_past, axis=1, keepdims=True), jnp.max(s_new, axis=1, keepdims=True))
    p_past = jnp.exp(s_past - m)
    p_new = jnp.exp(s_new - m)
    l = jnp.sum(p_past, axis=1, keepdims=True) + jnp.sum(p_new, axis=1, keepdims=True)
    return p_past / l, p_new / l


def _diff_decode_body(pt_ref, lam_ref, gh_ref, q_ref, kvn_ref, *rest, lam_init, n_pages):
    pages = rest[:n_pages]
    o_ref = rest[n_pages]
    t = q_ref.shape[0]
    past = n_pages * PAGE
    lam = _diff_lambda(lam_ref, lam_init)
    ti = _query_index((2 * t, LANES), t)
    lane = lax.broadcasted_iota(jnp.int32, (2 * t, LANES), 1)
    new_bias = jnp.where(lane <= ti, 0.0, NEG_INF)
    kpos_past = (lax.broadcasted_iota(jnp.int32, (1, past), 1) - past).astype(F32)
    kpos_new = lane.astype(F32)
    for h in range(H_A):
        cols = slice(h * DA, (h + 1) * DA)
        q1, q2 = _scaled_halves(q_ref[:, cols])
        qq = jnp.concatenate([q1, q2], axis=0)
        slope = _alibi_slope(h, H_A)
        s_past = jnp.concatenate(
            [_dot_nt(qq, pg[pl.ds(h, PAGE, stride=2 * H_A), :].astype(BF16)) for pg in pages], axis=1)
        s_past = s_past + slope * kpos_past
        k_new = _pad_rows(kvn_ref[:, cols], PAGE)
        v_new = _pad_rows(kvn_ref[:, W_A + h * DA:W_A + (h + 1) * DA], PAGE)
        s_new = _dot_nt(qq, k_new) + slope * kpos_new + new_bias
        p_past, p_new = _two_piece_softmax(s_past, s_new)
        a_past = (p_past[0:t] - lam * p_past[t:2 * t]).astype(BF16)
        a_new = (p_new[0:t] - lam * p_new[t:2 * t]).astype(BF16)
        o = _dot(a_new, v_new)
        for p, pg in enumerate(pages):
            o = o + _dot(a_past[:, p * PAGE:(p + 1) * PAGE], pg[pl.ds(H_A + h, PAGE, stride=2 * H_A), :].astype(BF16))
        o_ref[:, cols] = _head_norm(o, gh_ref[h:h + 1, :], lam_init)


def _page_specs(block, layer, n_pages):
    return [pl.BlockSpec((None, None) + block, functools.partial(lambda p, b, pt: (layer, pt[b, p], 0, 0), p))
            for p in range(n_pages)]


def _diff_decode(page_table, lam_vec, g_head, q_all, dkvb, cache, layer, t_dec, lam_init):
    n_req, n_pages = page_table.shape
    rows = PAGE * 2 * H_A
    grid_spec = pltpu.PrefetchScalarGridSpec(
        num_scalar_prefetch=1, grid=(n_req,),
        in_specs=[
            pl.BlockSpec((4, HEAD_DIM), lambda b, pt: (0, 0)),
            pl.BlockSpec((H_A, DA), lambda b, pt: (0, 0)),
            pl.BlockSpec((t_dec, W_A), lambda b, pt: (b, 0)),
            pl.BlockSpec((t_dec, 2 * W_A), lambda b, pt: (b, 0)),
        ] + _page_specs((rows, DA), layer, n_pages),
        out_specs=pl.BlockSpec((t_dec, W_A), lambda b, pt: (b, 0)),
    )
    return pl.pallas_call(
        functools.partial(_diff_decode_body, lam_init=lam_init, n_pages=n_pages),
        grid_spec=grid_spec, out_shape=jax.ShapeDtypeStruct((n_req * t_dec, W_A), F32),
        compiler_params=_params("arbitrary"), name="diff_decode",
    )(page_table, lam_vec, g_head, q_all, dkvb, *([cache] * n_pages))


def _fox_decode_body(pt_ref, q_ref, kvn_ref, lfn_ref, *rest, n_pages):
    pages = rest[:n_pages]
    lf_pages = rest[n_pages:2 * n_pages]
    o_ref = rest[2 * n_pages]
    t = q_ref.shape[0]
    past = n_pages * PAGE
    nh = 4
    d_all = _lane_cumsum(jnp.concatenate([lf[...] for lf in lf_pages] + [lfn_ref[...]], axis=1))
    ti = _query_index((nh * t, LANES), t)
    lane = lax.broadcasted_iota(jnp.int32, (nh * t, LANES), 1)
    new_bias = jnp.where(lane <= ti, 0.0, NEG_INF)
    lo, _ = _half_masks((t, LANES))
    for gp in range(KV_B // 2):
        heads = _pair_heads_b(gp)
        halves = [_scaled_halves(q_ref[:, tl * LANES:(tl + 1) * LANES]) for tl in (2 * gp, 2 * gp + 1)]
        qq = jnp.concatenate([halves[tl - 2 * gp][half] for tl, half, _ in heads], axis=0)
        d_rows = jnp.concatenate([jnp.broadcast_to(d_all[h:h + 1, :], (t, past + LANES)) for _, _, h in heads], axis=0)
        krows = slice(gp * LANES, (gp + 1) * LANES)
        vrows = slice(KV_B * HEAD_DIM + gp * LANES, KV_B * HEAD_DIM + (gp + 1) * LANES)
        s_past = jnp.concatenate([_dot(qq, pg[krows, :].astype(BF16)) for pg in pages], axis=1) - d_rows[:, :past]
        k_new = _pad_rows(kvn_ref[:, krows], PAGE)
        v_new = _pad_rows(kvn_ref[:, vrows], PAGE)
        s_new = _dot_nt(qq, k_new) - d_rows[:, past:] + new_bias
        p_past, p_new = _two_piece_softmax(s_past, s_new)
        p_past = p_past.astype(BF16)
        o = _dot(p_new.astype(BF16), v_new)
        for p, pg in enumerate(pages):
            o = o + _dot_nt(p_past[:, p * PAGE:(p + 1) * PAGE], pg[vrows, :].astype(BF16))
        o_ref[:, (2 * gp) * LANES:(2 * gp + 1) * LANES] = jnp.where(lo, o[0:t], o[t:2 * t])
        o_ref[:, (2 * gp + 1) * LANES:(2 * gp + 2) * LANES] = jnp.where(lo, o[2 * t:3 * t], o[3 * t:4 * t])


def _fox_decode(page_table, q_all, foxb, lfn, cache_kv, cache_lf, layer, t_dec):
    n_req, n_pages = page_table.shape
    grid_spec = pltpu.PrefetchScalarGridSpec(
        num_scalar_prefetch=1, grid=(n_req,),
        in_specs=[
            pl.BlockSpec((t_dec, W_B), lambda b, pt: (b, 1)),
            pl.BlockSpec((t_dec, 2 * KV_B * HEAD_DIM), lambda b, pt: (b, 0)),
            pl.BlockSpec((None, H_B, LANES), lambda b, pt: (b, 0, 0)),
        ] + _page_specs((2 * KV_B * HEAD_DIM, PAGE), layer, n_pages) + _page_specs((H_B, PAGE), layer, n_pages),
        out_specs=pl.BlockSpec((t_dec, W_B), lambda b, pt: (b, 0)),
    )
    return pl.pallas_call(
        functools.partial(_fox_decode_body, n_pages=n_pages),
        grid_spec=grid_spec, out_shape=jax.ShapeDtypeStruct((n_req * t_dec, W_B), F32),
        compiler_params=_params("arbitrary"), name="fox_decode",
    )(page_table, q_all, foxb, lfn, *([cache_kv] * n_pages), *([cache_lf] * n_pages))


def _nsa_decode_body(pt_ref, q_ref, kvn_ref, gc_ref, win_ref, pool_ref, *rest, n_pages):
    pages = rest[:n_pages]
    o_ref = rest[n_pages]
    t = q_ref.shape[0]
    past = n_pages * PAGE
    wbuf = win_ref.shape[1]
    nh = H_C // G_C
    n_blocks = (past + SEL_BLOCK) // SEL_BLOCK
    n_sel_lanes = 2 * LANES
    ti = _query_index((nh * t, LANES), t)
    lane = lax.broadcasted_iota(jnp.int32, (nh * t, LANES), 1)
    tq = past + ti

    ck = jnp.zeros((LANES, LANES), F32)
    cv = jnp.zeros((LANES, LANES), F32)
    for p, pg in enumerate(pages):
        ck = ck + _dot(pg[0:LANES, :].astype(BF16), pool_ref[p])
        cv = cv + _dot(pg[LANES:2 * LANES, :].astype(BF16), pool_ref[p])
    ckt = ck.astype(BF16)
    cvt = cv.astype(BF16)
    cpos = lane * CMP_STRIDE + (CMP_BLOCK - 1)
    cmp_ok = cpos <= tq
    cmp_bias = jnp.where(cmp_ok, 0.0, NEG_INF)
    cpos_rel = (cpos - past).astype(F32)

    kpos_past = lax.broadcasted_iota(jnp.int32, (1, past), 1)
    kpos_past_rel = (kpos_past - past).astype(F32)
    new_ok = lane <= ti
    kpos_new_rel = lane.astype(F32)
    wpos = past - wbuf + lax.broadcasted_iota(jnp.int32, (nh * t, wbuf), 1)
    tq_w = past + _query_index((nh * t, wbuf), t)
    win_bias = jnp.where((wpos <= tq_w) & (tq_w - wpos < WINDOW) & (wpos >= 0), 0.0, NEG_INF)
    wpos_rel = (wpos - past).astype(F32)
    win_new_bias = jnp.where(new_ok & (ti - lane < WINDOW), 0.0, NEG_INF)

    def gate_col(br, g):
        cols = [jax.nn.sigmoid(gc_ref[:, GC_LANE0 + br * H_C + nh * g + r:GC_LANE0 + br * H_C + nh * g + r + 1])
                for r in range(nh)]
        return jnp.concatenate(cols, axis=0)

    lo, hi = _half_masks((t, LANES))
    outs = []
    for g in range(G_C):
        qq = jnp.concatenate([_scaled_halves(q_ref[:, r * LANES:(r + 1) * LANES])[g] for r in range(nh)], axis=0)
        head_row = lax.broadcasted_iota(jnp.int32, (nh * t, 1), 0)
        slope = jnp.full((nh * t, 1), _alibi_slope(nh * g, H_C), F32)
        for r in range(1, nh):
            slope = jnp.where(head_row >= r * t, _alibi_slope(nh * g + r, H_C), slope)

        s = _dot(qq, ckt) + slope * cpos_rel + cmp_bias
        p = jnp.exp(s - jnp.max(s, axis=1, keepdims=True))
        p = jnp.where(cmp_ok, p / jnp.sum(p, axis=1, keepdims=True), 0.0)
        o = gate_col(0, g) * _dot_nt(p.astype(BF16), cvt)
        imp = p[0:t]
        for r in range(1, nh):
            imp = imp + p[r * t:(r + 1) * t]
        imp = jnp.concatenate([imp, jnp.zeros((t, n_sel_lanes - LANES), F32)], axis=1)

        qpos = past + lax.broadcasted_iota(jnp.int32, (t, n_sel_lanes), 0)
        score, valid, blk = _block_scores(imp, qpos, n_blocks)
        sel = jnp.where(_select_blocks(score, valid, blk, n_blocks), 1.0, 0.0)
        lo_t, _ = _half_masks((t, LANES))
        tiles = []
        for c in range(n_pages + 1):
            a = sel[:, 4 * (2 * c) + 3:4 * (2 * c) + 4]
            b = sel[:, 4 * (2 * c + 1) + 3:4 * (2 * c + 1) + 4] if 2 * c + 1 < n_blocks else jnp.zeros((t, 1), F32)
            tiles.append(jnp.where(lo_t, a, b))
        sel_keys = jnp.concatenate(tiles, axis=1)
        sel_keys = jnp.concatenate([sel_keys] * nh, axis=0)
        sel_bias_past = jnp.where(sel_keys[:, :past] > 0.5, 0.0, NEG_INF)
        sel_bias_new = jnp.where((sel_keys[:, past:] > 0.5) & new_ok, 0.0, NEG_INF)

        s_past = jnp.concatenate([_dot(qq, pg[2 * LANES:3 * LANES, :].astype(BF16)) for pg in pages], axis=1)
        s_past = s_past + slope * kpos_past_rel + sel_bias_past
        k_new = _pad_rows(kvn_ref[:, 2 * LANES:3 * LANES], PAGE)
        v_new = _pad_rows(kvn_ref[:, 3 * LANES:4 * LANES], PAGE)
        s_new = _dot_nt(qq, k_new) + slope * kpos_new_rel + sel_bias_new
        p_past, p_new = _two_piece_softmax(s_past, s_new)
        p_past = p_past.astype(BF16)
        osel = _dot(p_new.astype(BF16), v_new)
        for pi, pg in enumerate(pages):
            osel = osel + _dot_nt(p_past[:, pi * PAGE:(pi + 1) * PAGE], pg[3 * LANES:4 * LANES, :].astype(BF16))
        o = o + gate_col(1, g) * osel

        s_past = _dot(qq, win_ref[0:LANES, :].astype(BF16)) + slope * wpos_rel + win_bias
        k_new = _pad_rows(kvn_ref[:, 4 * LANES:5 * LANES], PAGE)
        v_new = _pad_rows(kvn_ref[:, 5 * LANES:6 * LANES], PAGE)
        s_new = _dot_nt(qq, k_new) + slope * kpos_new_rel + win_new_bias
        p_past, p_new = _two_piece_softmax(s_past, s_new)
        owin = _dot(p_new.astype(BF16), v_new) + _dot_nt(p_past.astype(BF16), win_ref[LANES:2 * LANES, :].astype(BF16))
        o = o + gate_col(2, g) * owin
        outs.append(o)

    for r in range(nh):
        o_ref[:, r * LANES:(r + 1) * LANES] = jnp.where(lo, outs[0][r * t:(r + 1) * t], outs[1][r * t:(r + 1) * t])


def _nsa_decode(page_table, q_all, nsab, fbgc, win_state, cache, layer, t_dec):
    n_req, n_pages = page_table.shape
    past = n_pages * PAGE
    wbuf = win_state.shape[-1]
    pool = jnp.asarray(_pool_matrix(past, LANES).reshape(n_pages, PAGE, LANES), BF16)
    grid_spec = pltpu.PrefetchScalarGridSpec(
        num_scalar_prefetch=1, grid=(n_req,),
        in_specs=[
            pl.BlockSpec((t_dec, W_C), lambda b, pt: (b, 2)),
            pl.BlockSpec((t_dec, 6 * G_C * HEAD_DIM), lambda b, pt: (b, 0)),
            pl.BlockSpec((t_dec, LANES), lambda b, pt: (b, 0)),
            pl.BlockSpec((None, None, 2 * G_C * HEAD_DIM, wbuf), lambda b, pt: (layer, b, 0, 0)),
            pl.BlockSpec((n_pages, PAGE, LANES), lambda b, pt: (0, 0, 0)),
        ] + _page_specs((4 * G_C * HEAD_DIM, PAGE), layer, n_pages),
        out_specs=pl.BlockSpec((t_dec, W_C), lambda b, pt: (b, 0)),
    )
    return pl.pallas_call(
        functools.partial(_nsa_decode_body, n_pages=n_pages),
        grid_spec=grid_spec, out_shape=jax.ShapeDtypeStruct((n_req * t_dec, W_C), F32),
        compiler_params=_params("arbitrary"), name="nsa_decode",
    )(page_table, q_all, nsab, fbgc, win_state, pool, *([cache] * n_pages))


def _finish_body(x_ref, oa_ref, ob_ref, oc_ref, gate_ref, wa_ref, wb_ref, wc_ref, wo_ref, g_ref, y_ref):
    m = gate_ref[:, 0:D_MODEL].astype(F32) * _dot(oa_ref[...].astype(BF16), wa_ref[...])
    m = m + gate_ref[:, D_MODEL:2 * D_MODEL].astype(F32) * _dot(ob_ref[...].astype(BF16), wb_ref[...])
    m = m + gate_ref[:, 2 * D_MODEL:3 * D_MODEL].astype(F32) * _dot(oc_ref[...].astype(BF16), wc_ref[...])
    y = _dot(m.astype(BF16), wo_ref[...])
    y_ref[...] = x_ref[...] + _rms(y, g_ref[...])


def _finish(x, oa, ob, oc, gate, wa, wb, wc, wo, g):
    n = x.shape[0]
    tm = min(TQ, n)
    row = lambda i: (i, 0)
    const = lambda i: (0, 0)
    return pl.pallas_call(
        _finish_body, grid=(n // tm,),
        in_specs=[
            pl.BlockSpec((tm, D_MODEL), row), pl.BlockSpec((tm, W_A), row), pl.BlockSpec((tm, W_B), row),
            pl.BlockSpec((tm, W_C), row), pl.BlockSpec((tm, 3 * D_MODEL), row),
            _resident((W_A, D_MODEL), const), _resident((W_B, D_MODEL), const), _resident((W_C, D_MODEL), const),
            _resident((D_MODEL, D_MODEL), const), pl.BlockSpec((1, D_MODEL), const),
        ],
        out_specs=pl.BlockSpec((tm, D_MODEL), row),
        out_shape=jax.ShapeDtypeStruct((n, D_MODEL), F32),
        compiler_params=_params("arbitrary"), name="finish",
    )(x, oa, ob, oc, gate, wa, wb, wc, wo, g)


def _mlp_body(x_ref, g2_ref, g3_ref, wu_ref, wd_ref, y_ref):
    x = x_ref[...]
    h = _rms(x, g2_ref[...]).astype(BF16)
    fc = D_MODEL
    d = jnp.zeros(x.shape, F32)
    for c in range(D_FF // fc):
        u = jnp.maximum(_dot(h, wu_ref[:, c * fc:(c + 1) * fc]), 0.0)
        d = d + _dot((u * u).astype(BF16), wd_ref[c * fc:(c + 1) * fc, :])
    y_ref[...] = x + _rms(d, g3_ref[...])


def _mlp(x, g2, g3, wu, wd):
    n = x.shape[0]
    tm = min(TQ, n)
    row = lambda i: (i, 0)
    const = lambda i: (0, 0)
    return pl.pallas_call(
        _mlp_body, grid=(n // tm,),
        in_specs=[
            pl.BlockSpec((tm, D_MODEL), row), pl.BlockSpec((1, D_MODEL), const), pl.BlockSpec((1, D_MODEL), const),
            _resident((D_MODEL, D_FF), const), _resident((D_FF, D_MODEL), const),
        ],
        out_specs=pl.BlockSpec((tm, D_MODEL), row),
        out_shape=jax.ShapeDtypeStruct((n, D_MODEL), F32),
        compiler_params=_params("arbitrary"), name="mlp",
    )(x, g2, g3, wu, wd)


def _prep_w_in(w_in):
    wt = jnp.transpose(w_in, (0, 2, 1))

    def rows(lo, n):
        return wt[:, lo:lo + n, :].astype(BF16)

    pieces = [rows(_OFF_KA, 2 * W_A), rows(_OFF_KB, 2 * KV_B * HEAD_DIM), rows(_OFF_KVC, 6 * G_C * HEAD_DIM),
              rows(_OFF_FB, H_B), rows(_OFF_GC, 3 * H_C),
              jnp.zeros((wt.shape[0], LANES - H_B - 3 * H_C, D_MODEL), BF16),
              rows(_OFF_QA, W_A)]
    pieces += [rows(_OFF_QB + h * HEAD_DIM, HEAD_DIM) for h in PERM_B]
    pieces += [rows(_OFF_QC + h * HEAD_DIM, HEAD_DIM) for h in PERM_C]
    pieces.append(rows(_OFF_GM, 3 * D_MODEL))
    out = jnp.concatenate(pieces, axis=1)
    assert out.shape[1] == NW
    return out


def _perm_rows(w, perm):
    return jnp.concatenate([w[:, h * HEAD_DIM:(h + 1) * HEAD_DIM, :] for h in perm], axis=1).astype(BF16)


def kernel(x_prompt, x_sample, cache_diff_kv, cache_fox_kv, cache_fox_logf, cache_nsa_kv, state_nsa_win_kv, page_table, w_in, b_f, diff_lam, diff_norm_g, w_branch_a, w_branch_b, w_branch_c, w_out, norm_g, w_up, w_down):
    n_batch, seq, _ = x_prompt.shape
    n_req, t_dec, _ = x_sample.shape
    depth = w_in.shape[0]
    n_phys = cache_diff_kv.shape[1]
    n_pages = page_table.shape[1]
    assert seq % TQ == 0

    wt_all = _prep_w_in(w_in)
    wa_all = w_branch_a.astype(BF16)
    wb_all = _perm_rows(w_branch_b, PERM_B)
    wc_all = _perm_rows(w_branch_c, PERM_C)
    wo_all = w_out.astype(BF16)
    wu_all = w_up.astype(BF16)
    wd_all = w_down.astype(BF16)

    c_diff = cache_diff_kv.reshape(depth, n_phys, PAGE * 2 * H_A, DA)
    c_fox = jnp.transpose(cache_fox_kv, (0, 1, 3, 4, 5, 2)).reshape(depth, n_phys, 2 * KV_B * HEAD_DIM, PAGE)
    c_lf = jnp.transpose(cache_fox_logf, (0, 1, 3, 2))
    c_nsa = jnp.transpose(cache_nsa_kv, (0, 1, 3, 4, 5, 2)).reshape(depth, n_phys, 4 * G_C * HEAD_DIM, PAGE)
    wbuf = state_nsa_win_kv.shape[2]
    c_win = jnp.transpose(state_nsa_win_kv, (0, 1, 3, 4, 5, 2)).reshape(depth, n_req, 2 * G_C * HEAD_DIM, wbuf)

    xp = x_prompt.reshape(n_batch * seq, D_MODEL)
    xs = x_sample.reshape(n_req * t_dec, D_MODEL)
    outs_p = [[] for _ in range(5)]
    outs_s = [[] for _ in range(5)]
    n_win = min(WINDOW, seq)
    for l in range(depth):
        lam_init = 0.8 - 0.6 * float(np.exp(-0.3 * l))
        g = norm_g[l]
        bf_col = b_f[l].reshape(H_B, 1)
        bf_row = jnp.pad(b_f[l], (0, LANES - H_B)).reshape(1, LANES)
        wt = wt_all[l]

        dkv, dkvb, foxt, foxtb, nsat, wint, nsatb, logft, gc, q_all, gate = _inproj_prompt(
            xp, g[0:1], bf_col, wt, n_batch, seq)
        oa = _diff_prompt(diff_lam[l], diff_norm_g[l], q_all, dkvb, n_batch, seq, lam_init)
        ob = _fox_prompt(q_all, foxtb, logft, n_batch, seq)
        oc = _nsa_prompt(q_all, nsatb, gc, n_batch, seq)
        xp = _finish(xp, oa, ob, oc, gate, wa_all[l], wb_all[l], wc_all[l], wo_all[l], g[1:2])
        xp = _mlp(xp, g[2:3], g[3:4], wu_all[l], wd_all[l])
        for lst, s in zip(outs_p, (dkv, foxt, logft, nsat, wint[:, :, seq - n_win:])):
            lst.append(s)

        dkv, fox, nsa, logft_s, fbgc, q_all, gate = _inproj_sample(xs, g[0:1], bf_col, bf_row, wt)
        lfn = jnp.pad(jnp.transpose(logft_s.reshape(H_B, n_req, t_dec), (1, 0, 2)), ((0, 0), (0, 0), (0, LANES - t_dec)))
        oa = _diff_decode(page_table, diff_lam[l], diff_norm_g[l], q_all, dkv, c_diff, l, t_dec, lam_init)
        ob = _fox_decode(page_table, q_all, fox, lfn, c_fox, c_lf, l, t_dec)
        oc = _nsa_decode(page_table, q_all, nsa, fbgc, c_win, c_nsa, l, t_dec)
        xs = _finish(xs, oa, ob, oc, gate, wa_all[l], wb_all[l], wc_all[l], wo_all[l], g[1:2])
        xs = _mlp(xs, g[2:3], g[3:4], wu_all[l], wd_all[l])
        for lst, s in zip(outs_s, (dkv, fox, fbgc[:, 0:H_B], nsa[:, 0:4 * G_C * HEAD_DIM], nsa[:, 4 * G_C * HEAD_DIM:])):
            lst.append(s)

    def tr(stack, dims):
        a = jnp.stack(stack)
        a = a.reshape(a.shape[:2] + dims + a.shape[3:])
        return jnp.moveaxis(a, -1, 2)

    dkv_p = jnp.stack(outs_p[0]).reshape(depth, n_batch, seq, 2, H_A, DA)
    fkv_p = tr(outs_p[1], (2, KV_B, HEAD_DIM))
    flf_p = tr(outs_p[2], (H_B,))
    nkv_p = tr(outs_p[3], (4, G_C, HEAD_DIM))
    nwin_p = tr(outs_p[4], (2, G_C, HEAD_DIM))
    dkv_s = jnp.stack(outs_s[0]).reshape(depth, n_req, t_dec, 2, H_A, DA)
    fkv_s = jnp.stack(outs_s[1]).reshape(depth, n_req, t_dec, 2, KV_B, HEAD_DIM)
    flf_s = jnp.stack(outs_s[2]).reshape(depth, n_req, t_dec, H_B)
    nkv_s = jnp.stack(outs_s[3]).reshape(depth, n_req, t_dec, 4, G_C, HEAD_DIM)
    nwin_s = jnp.stack(outs_s[4]).reshape(depth, n_req, t_dec, 2, G_C, HEAD_DIM)
    yp = xp.reshape(n_batch, seq, D_MODEL)
    ys = xs.reshape(n_req, t_dec, D_MODEL)
    return (yp, ys, dkv_p, dkv_s, fkv_p, fkv_s, flf_p, flf_s, nkv_p, nkv_s, nwin_p, nwin_s)
```

```python
import functools

import numpy as np
import jax
import jax.numpy as jnp
from jax import lax
from jax.experimental import pallas as pl
from jax.experimental.pallas import tpu as pltpu

F32 = jnp.float32
BF16 = jnp.bfloat16

D_MODEL = 1024
HEAD_DIM = 64
H_A = 4
DA = 2 * HEAD_DIM
W_A = H_A * DA
H_B = 8
KV_B = 4
W_B = H_B * HEAD_DIM
H_C = 8
G_C = 2
W_C = H_C * HEAD_DIM
CMP_STRIDE = 16
CMP_BLOCK = 2 * CMP_STRIDE
SEL_BLOCK = 64
N_SEL = 16
WINDOW = 512
D_FF = 4 * D_MODEL
PAGE = 128
RMS_EPS = 1e-6
NEG_INF = -1e30
FORCE_BONUS = 1e4
QK_SCALE = HEAD_DIM ** -0.5

LANES = 128
HALF = LANES // 2
TQ = 256
VMEM_LIMIT = 56 * 1024 * 1024

_OFF_QA, _OFF_KA, _OFF_VA = 0, W_A, 2 * W_A
_OFF_QB = 3 * W_A
_OFF_KB = _OFF_QB + W_B
_OFF_VB = _OFF_KB + KV_B * HEAD_DIM
_OFF_FB = _OFF_VB + KV_B * HEAD_DIM
_OFF_QC = _OFF_FB + H_B
_OFF_KVC = _OFF_QC + W_C
_OFF_GC = _OFF_KVC + 6 * G_C * HEAD_DIM
_OFF_GM = _OFF_GC + 3 * H_C
N_IN = _OFF_GM + 3 * D_MODEL

PERM_B = (0, 2, 1, 3, 4, 6, 5, 7)
PERM_C = (0, 4, 1, 5, 2, 6, 3, 7)

R_DIFF = 0
R_FOX = R_DIFF + 2 * W_A
R_NSA = R_FOX + 2 * KV_B * HEAD_DIM
R_FBGC = R_NSA + 6 * G_C * HEAD_DIM
R_Q = R_FBGC + LANES
R_GM = R_Q + W_A + W_B + W_C
NW = R_GM + 3 * D_MODEL
GC_LANE0 = H_B

NT_DIMS = (((1,), (1,)), ((), ()))


def _dot(a, b):
    return jnp.dot(a, b, preferred_element_type=F32)


def _dot_nt(a, b):
    return lax.dot_general(a, b, NT_DIMS, preferred_element_type=F32)


def _rms(x, g):
    return x * lax.rsqrt(jnp.mean(x * x, axis=-1, keepdims=True) + RMS_EPS) * g


def _log_sigmoid(x):
    return jnp.minimum(x, 0.0) - jnp.log1p(jnp.exp(-jnp.abs(x)))


def _tile_lanes(m, n):
    reps = n // m.shape[1]
    return m if reps == 1 else jnp.concatenate([m] * reps, axis=1)


def _params(*sem):
    return pltpu.CompilerParams(dimension_semantics=sem, vmem_limit_bytes=VMEM_LIMIT)


def _resident(shape, imap):
    return pl.BlockSpec(shape, imap, pipeline_mode=pl.Buffered(1))


def _inproj_prompt_body(x_ref, g_ref, bf_ref, w_ref, dkv_ref, ka_ref, vat_ref, qat_ref, foxt_ref, foxtb_ref,
                        nsat_ref, wint_ref, nsatb_ref, logft_ref, gc_ref, q_ref, gate_ref):
    h = _rms(x_ref[...], g_ref[...]).astype(BF16)

    def nn(lo, n):
        return _dot_nt(h, w_ref[lo:lo + n, :])

    def tt(lo, n):
        return _dot_nt(w_ref[lo:lo + n, :], h)

    z = nn(R_DIFF, 2 * W_A)
    dkv_ref[...] = z
    ka_ref[...] = z[:, 0:W_A].astype(BF16)
    vat_ref[...] = tt(R_DIFF + W_A, W_A).astype(BF16)
    qat_ref[...] = tt(R_Q, W_A).astype(BF16)
    z = tt(R_FOX, 2 * KV_B * HEAD_DIM)
    foxt_ref[...] = z
    foxtb_ref[...] = z.astype(BF16)
    z = tt(R_NSA, 6 * G_C * HEAD_DIM)
    nsat_ref[...] = z[0:4 * G_C * HEAD_DIM]
    wint_ref[...] = z[4 * G_C * HEAD_DIM:]
    nsatb_ref[...] = z.astype(BF16)
    z = tt(R_FBGC, 16)
    logft_ref[...] = _log_sigmoid(z[0:H_B] + bf_ref[...])
    gc_ref[...] = nn(R_FBGC, LANES)
    q_ref[...] = nn(R_Q + W_A, W_B + W_C).astype(BF16)
    gate_ref[...] = jax.nn.sigmoid(nn(R_GM, 3 * D_MODEL)).astype(BF16)


def _inproj_sample_body(x_ref, g_ref, bf_ref, bfrow_ref, w_ref, dkv_ref, fox_ref, nsa_ref, logft_ref, fbgc_ref,
                        q_ref, gate_ref):
    h = _rms(x_ref[...], g_ref[...]).astype(BF16)

    def nn(lo, n):
        return _dot_nt(h, w_ref[lo:lo + n, :])

    dkv_ref[...] = nn(R_DIFF, 2 * W_A)
    fox_ref[...] = nn(R_FOX, 2 * KV_B * HEAD_DIM)
    nsa_ref[...] = nn(R_NSA, 6 * G_C * HEAD_DIM)
    zt = _dot_nt(w_ref[R_FBGC:R_FBGC + 16, :], h)
    logft_ref[...] = _log_sigmoid(zt[0:H_B] + bf_ref[...])
    z = nn(R_FBGC, LANES)
    lane = lax.broadcasted_iota(jnp.int32, z.shape, 1)
    fbgc_ref[...] = jnp.where(lane < H_B, _log_sigmoid(z + bfrow_ref[...]), z)
    q_ref[...] = nn(R_Q, W_A + W_B + W_C)
    gate_ref[...] = jax.nn.sigmoid(nn(R_GM, 3 * D_MODEL)).astype(BF16)


def _inproj_prompt(x, g, bf_col, wt, n_batch, seq):
    n = x.shape[0]
    nq = seq // TQ
    grid = (n_batch, nq)
    kvw = KV_B * HEAD_DIM
    gw = G_C * HEAD_DIM
    row = lambda b, i: (b * nq + i, 0)
    const2 = lambda b, i: (0, 0)
    tr = lambda b, i: (b, 0, i)
    chunk = lambda b, i: (b, i, 0, 0)
    outs = (
        ((n, 2 * W_A), F32, (TQ, 2 * W_A), row),
        ((n, W_A), BF16, (TQ, W_A), row),
        ((n_batch, nq, W_A, TQ), BF16, (None, None, W_A, TQ), chunk),
        ((n_batch, nq, W_A, TQ), BF16, (None, None, W_A, TQ), chunk),
        ((n_batch, 2 * kvw, seq), F32, (None, 2 * kvw, TQ), tr),
        ((n_batch, nq, 2 * kvw, TQ), BF16, (None, None, 2 * kvw, TQ), chunk),
        ((n_batch, 4 * gw, seq), F32, (None, 4 * gw, TQ), tr),
        ((n_batch, 2 * gw, seq), F32, (None, 2 * gw, TQ), tr),
        ((n_batch, nq, 6 * gw, TQ), BF16, (None, None, 6 * gw, TQ), chunk),
        ((n_batch, H_B, seq), F32, (None, H_B, TQ), tr),
        ((n, LANES), F32, (TQ, LANES), row),
        ((n, W_B + W_C), BF16, (TQ, W_B + W_C), row),
        ((n, 3 * D_MODEL), BF16, (TQ, 3 * D_MODEL), row),
    )
    in_specs = [
        pl.BlockSpec((TQ, D_MODEL), row),
        pl.BlockSpec((1, D_MODEL), const2),
        pl.BlockSpec((H_B, 1), const2),
        _resident((NW, D_MODEL), const2),
    ]
    return pl.pallas_call(
        _inproj_prompt_body, grid=grid, in_specs=in_specs,
        out_specs=tuple(pl.BlockSpec(blk, imap) for _, _, blk, imap in outs),
        out_shape=tuple(jax.ShapeDtypeStruct(shp, dt) for shp, dt, _, _ in outs),
        compiler_params=_params("arbitrary", "arbitrary"), name="inproj_prompt",
    )(x, g, bf_col, wt)


def _inproj_sample(x, g, bf_col, bf_row, wt):
    n = x.shape[0]
    tm = min(TQ, n)
    grid = (n // tm,)
    row = lambda i: (i, 0)
    const2 = lambda i: (0, 0)
    widths = ((2 * W_A, F32), (512, F32), (768, F32))
    out_shape = tuple(jax.ShapeDtypeStruct((n, w), dt) for w, dt in widths) + (
        jax.ShapeDtypeStruct((H_B, n), F32),
        jax.ShapeDtypeStruct((n, LANES), F32),
        jax.ShapeDtypeStruct((n, W_A + W_B + W_C), F32),
        jax.ShapeDtypeStruct((n, 3 * D_MODEL), BF16),
    )
    out_specs = tuple(pl.BlockSpec((tm, w), row) for w, _ in widths) + (
        pl.BlockSpec((H_B, tm), lambda i: (0, i)),
        pl.BlockSpec((tm, LANES), row),
        pl.BlockSpec((tm, W_A + W_B + W_C), row),
        pl.BlockSpec((tm, 3 * D_MODEL), row),
    )
    in_specs = [
        pl.BlockSpec((tm, D_MODEL), row),
        pl.BlockSpec((1, D_MODEL), const2),
        pl.BlockSpec((H_B, 1), const2),
        pl.BlockSpec((1, LANES), const2),
        _resident((NW, D_MODEL), const2),
    ]
    return pl.pallas_call(
        _inproj_sample_body, grid=grid, in_specs=in_specs, out_specs=out_specs, out_shape=out_shape,
        compiler_params=_params("arbitrary"), name="inproj_sample",
    )(x, g, bf_col, bf_row, wt)


def _flash_init(m_scr, l_scr, acc_scr):
    m_scr[...] = jnp.full(m_scr.shape, NEG_INF, F32)
    l_scr[...] = jnp.zeros(l_scr.shape, F32)
    acc_scr[...] = jnp.zeros(acc_scr.shape, F32)


def _flash_update(s, idx, m_scr, l_scr, acc_scr, pv):
    m_prev = m_scr[idx]
    m_next = jnp.maximum(m_prev, jnp.max(s, axis=1, keepdims=True))
    alpha = jnp.exp(m_prev - m_next)
    p = jnp.exp(s - _tile_lanes(m_next, s.shape[1]))
    l_scr[idx] = alpha * l_scr[idx] + jnp.sum(p, axis=1, keepdims=True)
    m_scr[idx] = m_next
    acc_scr[idx] = alpha * acc_scr[idx] + pv(p.astype(BF16))


def _causal_bias(n):
    r = lax.broadcasted_iota(jnp.int32, (n, n), 0)
    c = lax.broadcasted_iota(jnp.int32, (n, n), 1)
    return jnp.where(c <= r, 0.0, NEG_INF).astype(F32)


def _half_masks(shape):
    lane = lax.broadcasted_iota(jnp.int32, shape, 1)
    return lane < HALF, lane >= HALF


def _scaled_halves(q_tile):
    q = q_tile.astype(F32) * QK_SCALE
    lo, hi = _half_masks(q.shape)
    return jnp.where(lo, q, 0.0).astype(BF16), jnp.where(hi, q, 0.0).astype(BF16)


def _diff_lambda(lam_ref, lam_init):
    lv = lam_ref[...]
    a = jnp.sum(lv[0:1] * lv[1:2], axis=1, keepdims=True)
    b = jnp.sum(lv[2:3] * lv[3:4], axis=1, keepdims=True)
    return jnp.exp(a) - jnp.exp(b) + lam_init


def _head_norm(o, g_row, lam_init):
    return o * lax.rsqrt(jnp.mean(o * o, axis=-1, keepdims=True) + RMS_EPS) * g_row * (1.0 - lam_init)


def _alibi_slope(h, n_heads):
    return float(2.0 ** (-8.0 * (h + 1) / n_heads))


def _diff_prompt_body(lam_ref, gh_ref, q_ref, k_ref, v_ref, o_ref, m_scr, l_scr, acc_scr, *, lam_init):
    i = pl.program_id(1)
    tq = q_ref.shape[0]
    lam = _diff_lambda(lam_ref, lam_init)
    tri = _causal_bias(tq)
    kiota = lax.broadcasted_iota(jnp.int32, (1, tq), 1)
    for h in range(H_A):
        cols = slice(h * DA, (h + 1) * DA)
        q1, q2 = _scaled_halves(q_ref[:, cols])
        slope = _alibi_slope(h, H_A)
        _flash_init(m_scr, l_scr, acc_scr)

        def step(j, extra, q1=q1, q2=q2, slope=slope, cols=cols):
            rows = pl.ds(pl.multiple_of(j * tq, tq), tq)
            kc = k_ref[rows, cols]
            vc = v_ref[rows, cols]
            bias = slope * ((j - i) * tq + kiota).astype(F32)
            if extra is not None:
                bias = bias + extra
            for mi, qm in enumerate((q1, q2)):
                _flash_update(_dot_nt(qm, kc) + bias, mi, m_scr, l_scr, acc_scr, lambda p: _dot(p, vc))

        def body(j, c):
            step(j, None)
            return c

        lax.fori_loop(0, i, body, 0)
        step(i, tri)
        o = acc_scr[0] / l_scr[0] - lam * (acc_scr[1] / l_scr[1])
        o_ref[:, cols] = _head_norm(o, gh_ref[h:h + 1, :], lam_init).astype(BF16)


def _diff_prompt(lam_vec, g_head, q_all, kv_b, n_batch, seq, lam_init):
    nq = seq // TQ
    n = q_all.shape[0]
    return pl.pallas_call(
        functools.partial(_diff_prompt_body, lam_init=lam_init),
        grid=(n_batch, nq),
        in_specs=[
            pl.BlockSpec((4, HEAD_DIM), lambda b, i: (0, 0)),
            pl.BlockSpec((H_A, DA), lambda b, i: (0, 0)),
            pl.BlockSpec((TQ, W_A), lambda b, i: (b * nq + i, 0)),
            pl.BlockSpec((seq, W_A), lambda b, i: (b, 0)),
            pl.BlockSpec((seq, W_A), lambda b, i: (b, 1)),
        ],
        out_specs=pl.BlockSpec((TQ, W_A), lambda b, i: (b * nq + i, 0)),
        out_shape=jax.ShapeDtypeStruct((n, W_A), BF16),
        scratch_shapes=[pltpu.VMEM((2, TQ, LANES), F32)] * 3,
        compiler_params=_params("arbitrary", "arbitrary"), name="diff_prompt",
    )(lam_vec, g_head, q_all, kv_b, kv_b)


def _lane_cumsum(x):
    n = x.shape[1]
    lane = lax.broadcasted_iota(jnp.int32, x.shape, 1)
    sh = 1
    while sh < n:
        x = x + jnp.where(lane >= sh, pltpu.roll(x, sh, axis=1), 0.0)
        sh *= 2
    return x


def _pair_heads_b(gp):
    return ((2 * gp, 0, 4 * gp), (2 * gp, 1, 4 * gp + 2), (2 * gp + 1, 0, 4 * gp + 1), (2 * gp + 1, 1, 4 * gp + 3))


def _fox_prompt_body(q_ref, kvt_ref, lf_ref, o_ref, d_scr, m_scr, l_scr, acc_scr):
    i = pl.program_id(1)
    tq = q_ref.shape[0]
    nchunk = kvt_ref.shape[0]

    @pl.when(i == 0)
    def _():
        d = _lane_cumsum(lf_ref[...])
        for c in range(nchunk):
            d_scr[c] = d[:, c * tq:(c + 1) * tq]

    tri = _causal_bias(tq)
    lo, _ = _half_masks((tq, LANES))
    for gp in range(KV_B // 2):
        heads = _pair_heads_b(gp)
        halves = [_scaled_halves(q_ref[:, t * LANES:(t + 1) * LANES]) for t in (2 * gp, 2 * gp + 1)]
        qms = [halves[t - 2 * gp][half] for t, half, _ in heads]
        _flash_init(m_scr, l_scr, acc_scr)
        krows = slice(gp * LANES, (gp + 1) * LANES)
        vrows = slice(KV_B * HEAD_DIM + gp * LANES, KV_B * HEAD_DIM + (gp + 1) * LANES)

        def step(j, extra, qms=qms, heads=heads, krows=krows, vrows=vrows):
            ktc = kvt_ref[j, krows, :]
            vtc = kvt_ref[j, vrows, :]
            dj = d_scr[j]
            for idx, (_, _, h) in enumerate(heads):
                s = _dot(qms[idx], ktc) - dj[h:h + 1, :]
                if extra is not None:
                    s = s + extra
                _flash_update(s, idx, m_scr, l_scr, acc_scr, lambda p: _dot_nt(p, vtc))

        def body(j, c):
            step(j, None)
            return c

        lax.fori_loop(0, i, body, 0)
        step(i, tri)
        outs = [acc_scr[idx] / l_scr[idx] for idx in range(4)]
        o_ref[:, (2 * gp) * LANES:(2 * gp + 1) * LANES] = jnp.where(lo, outs[0], outs[1]).astype(BF16)
        o_ref[:, (2 * gp + 1) * LANES:(2 * gp + 2) * LANES] = jnp.where(lo, outs[2], outs[3]).astype(BF16)


def _fox_prompt(q_all, foxtb, logft, n_batch, seq):
    nq = seq // TQ
    n = q_all.shape[0]
    return pl.pallas_call(
        _fox_prompt_body,
        grid=(n_batch, nq),
        in_specs=[
            pl.BlockSpec((TQ, W_B), lambda b, i: (b * nq + i, 0)),
            pl.BlockSpec((None, nq, 512, TQ), lambda b, i: (b, 0, 0, 0)),
            pl.BlockSpec((None, H_B, seq), lambda b, i: (b, 0, 0)),
        ],
        out_specs=pl.BlockSpec((TQ, W_B), lambda b, i: (b * nq + i, 0)),
        out_shape=jax.ShapeDtypeStruct((n, W_B), BF16),
        scratch_shapes=[pltpu.VMEM((nq, H_B, TQ), F32)] + [pltpu.VMEM((4, TQ, LANES), F32)] * 3,
        compiler_params=_params("arbitrary", "arbitrary"), name="fox_prompt",
    )(q_all, foxtb, logft)


def _block_scores(imp, qpos, n_blocks):
    lane = lax.broadcasted_iota(jnp.int32, imp.shape, 1)
    y = imp + pltpu.roll(imp, 1, axis=1)
    bs = y + pltpu.roll(y, 2, axis=1)
    blk = lane >> 2
    valid = ((lane & 3) == 3) & (blk < n_blocks) & (blk * SEL_BLOCK <= qpos)
    cur = qpos >> 6
    forced = (blk == 0) | (blk == cur) | (blk == cur - 1)
    score = jnp.where(valid, bs + FORCE_BONUS * jnp.where(forced, 1.0, 0.0), NEG_INF)
    return score, valid, blk


def _select_blocks(score, valid, blk, n_blocks):
    cnt = jnp.zeros(score.shape, F32)
    for jp in range(n_blocks):
        v = score[:, 4 * jp + 3:4 * jp + 4]
        better = (v > score) | ((v == score) & (blk > jp))
        cnt = cnt + jnp.where(better, 1.0, 0.0)
    return valid & (cnt < float(min(N_SEL, n_blocks)))


def _nsa_prompt_body(q_ref, kvt_ref, gc_ref, pool_ref, e_ref, o_ref, ck_scr, cv_scr, selb_scr, winb_scr,
                     m_scr, l_scr, acc_scr, oc_scr, imp_scr):
    i = pl.program_id(1)
    tq = q_ref.shape[0]
    nchunk = kvt_ref.shape[0]
    n_blocks = nchunk * tq // SEL_BLOCK
    q0 = i * tq
    lane_b = lax.broadcasted_iota(jnp.int32, (n_blocks, LANES), 1)

    @pl.when(i == 0)
    def _():
        ck = jnp.zeros((LANES, LANES), F32)
        cv = jnp.zeros((LANES, LANES), F32)
        for c in range(nchunk):
            ck = ck + _dot(kvt_ref[c, 0:LANES, :], pool_ref[c])
            cv = cv + _dot(kvt_ref[c, LANES:2 * LANES, :], pool_ref[c])
        ck_scr[...] = ck.astype(BF16)
        cv_scr[...] = cv.astype(BF16)

    row = lax.broadcasted_iota(jnp.int32, (tq, LANES), 0)
    lane = lax.broadcasted_iota(jnp.int32, (tq, LANES), 1)
    qpos = q0 + row
    cpos = lane * CMP_STRIDE + (CMP_BLOCK - 1)
    cmp_ok = cpos <= qpos
    cmp_bias = jnp.where(cmp_ok, 0.0, NEG_INF)
    cpos_rel = (cpos - q0).astype(F32)

    row_k = lax.broadcasted_iota(jnp.int32, (tq, tq), 0) + q0
    col_k = lax.broadcasted_iota(jnp.int32, (tq, tq), 1)
    kiota = lax.broadcasted_iota(jnp.int32, (1, tq), 1)
    n_win = WINDOW // tq + 1
    for slot in range(n_win):
        kpos = (i - (n_win - 1) + slot) * tq + col_k
        ok = (kpos <= row_k) & (row_k - kpos < WINDOW) & (kpos >= 0)
        winb_scr[slot] = jnp.where(ok, 0.0, NEG_INF)

    def gate(br, h):
        c = GC_LANE0 + br * H_C + h
        return jax.nn.sigmoid(gc_ref[:, c:c + 1])

    lo, hi = _half_masks((tq, LANES))
    for g in range(G_C):
        heads = [(H_C // G_C) * g + r for r in range(H_C // G_C)]
        qms = [_scaled_halves(q_ref[:, r * LANES:(r + 1) * LANES])[g] for r in range(len(heads))]
        slopes = [_alibi_slope(h, H_C) for h in heads]

        imp = jnp.zeros((tq, LANES), F32)
        ckt = ck_scr[...]
        cvt = cv_scr[...]
        for r, h in enumerate(heads):
            s = _dot(qms[r], ckt) + slopes[r] * cpos_rel + cmp_bias
            p = jnp.exp(s - jnp.max(s, axis=1, keepdims=True))
            p = jnp.where(cmp_ok, p / jnp.sum(p, axis=1, keepdims=True), 0.0)
            imp = imp + p
            oc_scr[h] = gate(0, h) * _dot_nt(p.astype(BF16), cvt)

        impt = imp.T
        for t in range(tq // LANES):
            imp_scr[t] = impt[:, t * LANES:(t + 1) * LANES]
        bias_t = jnp.concatenate(
            [_select_blocks_t(imp_scr, t, q0 + t * LANES + lane_b, n_blocks) for t in range(tq // LANES)],
            axis=1)
        sel_t = jnp.where(bias_t > 0.5 * NEG_INF, 1.0, 0.0)
        sel = jnp.concatenate([sel_t, jnp.zeros((LANES - n_blocks, tq), F32)], axis=0).T.astype(BF16)

        def mk_bias(c, carry, sel=sel):
            ex = _dot(sel, e_ref[c])
            ok = (ex > 0.5) & (c * tq + col_k <= row_k)
            selb_scr[c] = jnp.where(ok, 0.0, NEG_INF)
            return carry

        lax.fori_loop(0, i + 1, mk_bias, 0)

        _flash_init(m_scr, l_scr, acc_scr)

        def sel_step(c, carry, qms=qms, slopes=slopes):
            ktc = kvt_ref[c, 2 * LANES:3 * LANES, :]
            vtc = kvt_ref[c, 3 * LANES:4 * LANES, :]
            kpos_rel = ((c - i) * tq + kiota).astype(F32)
            mb = selb_scr[c]
            for r in range(len(qms)):
                s = _dot(qms[r], ktc) + slopes[r] * kpos_rel + mb
                _flash_update(s, r, m_scr, l_scr, acc_scr, lambda p: _dot_nt(p, vtc))
            return carry

        lax.fori_loop(0, i + 1, sel_step, 0)
        for r, h in enumerate(heads):
            oc_scr[h] = oc_scr[h] + gate(1, h) * (acc_scr[r] / l_scr[r])

        _flash_init(m_scr, l_scr, acc_scr)

        def win_step(slot, carry, qms=qms, slopes=slopes):
            c = i - (n_win - 1) + slot
            ktc = kvt_ref[c, 4 * LANES:5 * LANES, :]
            vtc = kvt_ref[c, 5 * LANES:6 * LANES, :]
            kpos_rel = ((c - i) * tq + kiota).astype(F32)
            mb = winb_scr[slot]
            for r in range(len(qms)):
                s = _dot(qms[r], ktc) + slopes[r] * kpos_rel + mb
                _flash_update(s, r, m_scr, l_scr, acc_scr, lambda p: _dot_nt(p, vtc))
            return carry

        lax.fori_loop(jnp.maximum(n_win - 1 - i, 0), n_win, win_step, 0)
        for r, h in enumerate(heads):
            oc_scr[h] = oc_scr[h] + gate(2, h) * (acc_scr[r] / l_scr[r])

    for t in range(H_C // G_C):
        o_ref[:, t * LANES:(t + 1) * LANES] = jnp.where(lo, oc_scr[t], oc_scr[t + H_C // G_C]).astype(BF16)


def _pool_matrix(length, n_cols):
    t = np.arange(length)[:, None]
    c = np.arange(n_cols)[None, :]
    m = (t >= c * CMP_STRIDE) & (t < c * CMP_STRIDE + CMP_BLOCK)
    return (m.astype(np.float32) / CMP_BLOCK)


def _expand_matrix(n_lanes, length):
    r = np.arange(n_lanes)[:, None]
    k = np.arange(length)[None, :]
    return (r == k // SEL_BLOCK).astype(np.float32)


def _nsa_prompt(q_all, nsatb, gc, n_batch, seq):
    nq = seq // TQ
    n = q_all.shape[0]
    pool = jnp.asarray(_pool_matrix(seq, LANES).reshape(nq, TQ, LANES), BF16)
    expand = jnp.asarray(_expand_matrix(LANES, seq).reshape(LANES, nq, TQ).transpose(1, 0, 2), BF16)
    n_win = WINDOW // TQ + 1
    return pl.pallas_call(
        _nsa_prompt_body,
        grid=(n_batch, nq),
        in_specs=[
            pl.BlockSpec((TQ, W_C), lambda b, i: (b * nq + i, 1)),
            pl.BlockSpec((None, nq, 768, TQ), lambda b, i: (b, 0, 0, 0)),
            pl.BlockSpec((TQ, LANES), lambda b, i: (b * nq + i, 0)),
            pl.BlockSpec((nq, TQ, LANES), lambda b, i: (0, 0, 0)),
            pl.BlockSpec((nq, LANES, TQ), lambda b, i: (0, 0, 0)),
        ],
        out_specs=pl.BlockSpec((TQ, W_C), lambda b, i: (b * nq + i, 0)),
        out_shape=jax.ShapeDtypeStruct((n, W_C), BF16),
        scratch_shapes=[
            pltpu.VMEM((LANES, LANES), BF16), pltpu.VMEM((LANES, LANES), BF16),
            pltpu.VMEM((nq, TQ, TQ), F32), pltpu.VMEM((n_win, TQ, TQ), F32),
            pltpu.VMEM((H_C // G_C, TQ, LANES), F32), pltpu.VMEM((H_C // G_C, TQ, LANES), F32),
            pltpu.VMEM((H_C // G_C, TQ, LANES), F32), pltpu.VMEM((H_C, TQ, LANES), F32),
            pltpu.VMEM((TQ // LANES, LANES, LANES), F32),
        ],
        compiler_params=_params("arbitrary", "arbitrary"), name="nsa_prompt",
    )(q_all, nsatb, gc, pool, expand)


AUG_POS = 0
AUG_BLK = 16
POS_RADIX = 256


def _t_flash_init(m_scr, l_scr, acc_scr):
    m_scr[...] = jnp.full(m_scr.shape, NEG_INF, F32)
    l_scr[...] = jnp.zeros(l_scr.shape, F32)
    acc_scr[...] = jnp.zeros(acc_scr.shape, F32)


def _t_flash_update(st, idx, m_scr, l_scr, acc_scr, vt):
    m_prev = m_scr[idx]
    m_next = jnp.maximum(m_prev, jnp.max(st, axis=0, keepdims=True))
    alpha = jnp.exp(m_prev - m_next)
    p = jnp.exp(st - m_next)
    l_scr[idx] = alpha * l_scr[idx] + jnp.sum(p, axis=0, keepdims=True)
    m_scr[idx] = m_next
    acc_scr[idx] = alpha * acc_scr[idx] + _dot(vt, p.astype(BF16))


def _causal_bias_t(n):
    k = lax.broadcasted_iota(jnp.int32, (n, n), 0)
    q = lax.broadcasted_iota(jnp.int32, (n, n), 1)
    return jnp.where(k <= q, 0.0, NEG_INF).astype(F32)


def _scale_q(qt):
    return (qt.astype(F32) * QK_SCALE).astype(BF16)


def _alibi_rows(slope, tq, block_bias=None):
    row = lax.broadcasted_iota(jnp.int32, (AUG_BLK, tq), 0)
    head = jnp.where(row == AUG_POS, POS_RADIX * slope, jnp.where(row == AUG_POS + 1, slope, 0.0)).astype(BF16)
    if block_bias is None:
        return jnp.concatenate([head, jnp.zeros((LANES - AUG_BLK, tq), BF16)], axis=0)
    nb = block_bias.shape[0]
    return jnp.concatenate([head, block_bias.astype(BF16), jnp.zeros((LANES - AUG_BLK - nb, tq), BF16)], axis=0)


def _pos_aug(pos, n_blocks=0):
    a = np.zeros((pos.shape[0], LANES), np.float32)
    a[:, AUG_POS] = pos // POS_RADIX
    a[:, AUG_POS + 1] = pos % POS_RADIX
    if n_blocks:
        assert AUG_BLK + n_blocks <= LANES
        a[np.arange(pos.shape[0]), AUG_BLK + pos // SEL_BLOCK] = 1.0
    return a


def _diff_prompt_t_body(lam_ref, gh_ref, qt_ref, k_ref, vt_ref, kaug_ref, o_ref, qa_scr, m_scr, l_scr, acc_scr, *,
                        lam_init):
    i = pl.program_id(1)
    tq = qt_ref.shape[1]
    lam = _diff_lambda(lam_ref, lam_init)
    tri = _causal_bias_t(tq)
    zeros = jnp.zeros((HEAD_DIM, tq), BF16)
    for h in range(H_A):
        qh = _scale_q(qt_ref[h * DA:(h + 1) * DA, :])
        aug = _alibi_rows(_alibi_slope(h, H_A), tq)
        qa_scr[h, :, 0:tq] = jnp.concatenate([qh[0:HEAD_DIM], zeros, aug], axis=0)
        qa_scr[h, :, tq:2 * tq] = jnp.concatenate([zeros, qh[HEAD_DIM:DA], aug], axis=0)
    _t_flash_init(m_scr, l_scr, acc_scr)

    def step(j, extra):
        rows = pl.ds(pl.multiple_of(j * tq, tq), tq)
        kaug = kaug_ref[j]
        for h in range(H_A):
            ka = jnp.concatenate([k_ref[rows, h * DA:(h + 1) * DA], kaug], axis=1)
            st2 = _dot(ka, qa_scr[h])
            vt = vt_ref[j, h * DA:(h + 1) * DA, :]
            for mi in range(2):
                st = st2[:, mi * tq:(mi + 1) * tq]
                if extra is not None:
                    st = st + extra
                _t_flash_update(st, 2 * h + mi, m_scr, l_scr, acc_scr, vt)

    def body(j, c):
        step(j, None)
        return c

    lax.fori_loop(0, i, body, 0)
    step(i, tri)
    for h in range(H_A):
        ot = acc_scr[2 * h] / l_scr[2 * h] - lam * (acc_scr[2 * h + 1] / l_scr[2 * h + 1])
        o_ref[:, h * DA:(h + 1) * DA] = _head_norm(ot.T, gh_ref[h:h + 1, :], lam_init).astype(BF16)


def _diff_prompt_t(lam_vec, g_head, qt, ka, vat, n_batch, seq, lam_init):
    nq = seq // TQ
    n = ka.shape[0]
    kaug = jnp.asarray(_pos_aug(np.arange(seq)).reshape(nq, TQ, LANES), BF16)
    return pl.pallas_call(
        functools.partial(_diff_prompt_t_body, lam_init=lam_init),
        grid=(n_batch, nq),
        in_specs=[
            pl.BlockSpec((4, HEAD_DIM), lambda b, i: (0, 0)),
            pl.BlockSpec((H_A, DA), lambda b, i: (0, 0)),
            pl.BlockSpec((None, None, W_A, TQ), lambda b, i: (b, i, 0, 0)),
            pl.BlockSpec((seq, W_A), lambda b, i: (b, 0)),
            pl.BlockSpec((None, nq, W_A, TQ), lambda b, i: (b, 0, 0, 0)),
            pl.BlockSpec((nq, TQ, LANES), lambda b, i: (0, 0, 0)),
        ],
        out_specs=pl.BlockSpec((TQ, W_A), lambda b, i: (b * nq + i, 0)),
        out_shape=jax.ShapeDtypeStruct((n, W_A), BF16),
        scratch_shapes=[pltpu.VMEM((H_A, 2 * LANES, 2 * TQ), BF16), pltpu.VMEM((2 * H_A, 1, TQ), F32),
                        pltpu.VMEM((2 * H_A, 1, TQ), F32), pltpu.VMEM((2 * H_A, DA, TQ), F32)],
        compiler_params=_params("arbitrary", "arbitrary"), name="diff_prompt",
    )(lam_vec, g_head, qt, ka, vat, kaug)


def _sublane_cumsum(x):
    n = x.shape[0]
    row = lax.broadcasted_iota(jnp.int32, x.shape, 0)
    sh = 1
    while sh < n:
        x = x + jnp.where(row >= sh, pltpu.roll(x, sh, axis=0), 0.0)
        sh *= 2
    return x


def _split3(x):
    hi = x.astype(BF16).astype(F32)
    r = x - hi
    mid = r.astype(BF16).astype(F32)
    lo = (r - mid).astype(BF16).astype(F32)
    return hi, mid, lo


def _fox_prompt_t_body(qt_ref, k_ref, vt_ref, lf_ref, o_ref, daug_scr, qa_scr, m_scr, l_scr, acc_scr, ot_scr):
    i = pl.program_id(1)
    tq = qt_ref.shape[1]
    nchunk = vt_ref.shape[0]

    @pl.when(i == 0)
    def _():
        hi, mid, lo = _split3(_sublane_cumsum(lf_ref[...]))
        lane = lax.broadcasted_iota(jnp.int32, hi.shape, 1)
        d = jnp.where(lane < H_B, hi, 0.0)
        d = d + jnp.where((lane >= H_B) & (lane < 2 * H_B), pltpu.roll(mid, H_B, axis=1), 0.0)
        d = d + jnp.where((lane >= 2 * H_B) & (lane < 3 * H_B), pltpu.roll(lo, 2 * H_B, axis=1), 0.0)
        d = d.astype(BF16)
        for c in range(nchunk):
            daug_scr[c] = d[c * tq:(c + 1) * tq, :]

    tri = _causal_bias_t(tq)
    zeros = jnp.zeros((HEAD_DIM, tq), BF16)
    row = lax.broadcasted_iota(jnp.int32, (LANES, tq), 0)
    for h in range(H_B):
        g = h // (H_B // KV_B)
        pos = PERM_B.index(h)
        qh = _scale_q(qt_ref[pos * HEAD_DIM:(pos + 1) * HEAD_DIM, :])
        aug = jnp.where((row == h) | (row == H_B + h) | (row == 2 * H_B + h), -1.0, 0.0).astype(BF16)
        qa_scr[h] = jnp.concatenate([qh, zeros, aug] if g % 2 == 0 else [zeros, qh, aug], axis=0)
    _t_flash_init(m_scr, l_scr, acc_scr)

    def step(j, extra):
        rows = pl.ds(pl.multiple_of(j * tq, tq), tq)
        daug = daug_scr[j]
        for gp in range(KV_B // 2):
            ka = jnp.concatenate([k_ref[rows, gp * LANES:(gp + 1) * LANES], daug], axis=1)
            for h in range(4 * gp, 4 * gp + 4):
                g = h // (H_B // KV_B)
                st = _dot(ka, qa_scr[h])
                if extra is not None:
                    st = st + extra
                _t_flash_update(st, h, m_scr, l_scr, acc_scr, vt_ref[j, g * HEAD_DIM:(g + 1) * HEAD_DIM, :])

    def body(j, c):
        step(j, None)
        return c

    lax.fori_loop(0, i, body, 0)
    step(i, tri)
    for h in range(H_B):
        pos = PERM_B.index(h)
        ot_scr[pos * HEAD_DIM:(pos + 1) * HEAD_DIM, :] = acc_scr[h] / l_scr[h]
    o_ref[...] = ot_scr[...].T.astype(BF16)


def _fox_prompt_t(qt, kb, vbt, lfn, n_batch, seq):
    nq = seq // TQ
    n = kb.shape[0]
    kvw = KV_B * HEAD_DIM
    return pl.pallas_call(
        _fox_prompt_t_body,
        grid=(n_batch, nq),
        in_specs=[
            pl.BlockSpec((None, None, W_B, TQ), lambda b, i: (b, i, 1, 0)),
            pl.BlockSpec((seq, kvw), lambda b, i: (b, 0)),
            pl.BlockSpec((None, nq, kvw, TQ), lambda b, i: (b, 0, 0, 0)),
            pl.BlockSpec((seq, LANES), lambda b, i: (b, 0)),
        ],
        out_specs=pl.BlockSpec((TQ, W_B), lambda b, i: (b * nq + i, 0)),
        out_shape=jax.ShapeDtypeStruct((n, W_B), BF16),
        scratch_shapes=[pltpu.VMEM((nq, TQ, LANES), BF16), pltpu.VMEM((H_B, 2 * LANES, TQ), BF16),
                        pltpu.VMEM((H_B, 1, TQ), F32), pltpu.VMEM((H_B, 1, TQ), F32),
                        pltpu.VMEM((H_B, HEAD_DIM, TQ), F32), pltpu.VMEM((W_B, TQ), F32)],
        compiler_params=_params("arbitrary", "arbitrary"), name="fox_prompt",
    )(qt, kb, vbt, lfn)


def _select_blocks_t(imp_scr, t, qpos, n_blocks):
    per = SEL_BLOCK // CMP_STRIDE
    bs = imp_scr[t, pl.ds(0, n_blocks, stride=per), :]
    for j in range(1, per):
        bs = bs + imp_scr[t, pl.ds(j, n_blocks, stride=per), :]
    blk = lax.broadcasted_iota(jnp.int32, bs.shape, 0)
    valid = blk * SEL_BLOCK <= qpos
    cur = qpos >> 6
    forced = (blk == 0) | (blk == cur) | (blk == cur - 1)
    score = jnp.where(valid, bs + FORCE_BONUS * jnp.where(forced, 1.0, 0.0), NEG_INF)
    cnt = jnp.zeros(bs.shape, F32)
    for jp in range(n_blocks):
        v = score[jp:jp + 1, :]
        better = (v > score) | ((v == score) & (blk > jp))
        cnt = cnt + jnp.where(better, 1.0, 0.0)
    return jnp.where(valid & (cnt < float(min(N_SEL, n_blocks))), 0.0, NEG_INF)


def _nsa_prompt_t_body(qt_ref, k_ref, vt_ref, gct_ref, pool_ref, poolt_ref, caug_ref, kpos_ref, ksel_ref, o_ref,
                       ck_scr, cvt_scr, imp_scr, qa_scr, m_scr, l_scr, acc_scr, oc_scr, ot_scr):
    i = pl.program_id(1)
    tq = qt_ref.shape[1]
    nchunk = vt_ref.shape[0]
    n_blocks = nchunk * tq // SEL_BLOCK
    n_cmp = LANES
    gw = G_C * HEAD_DIM
    nh = H_C // G_C
    q0 = i * tq
    assert WINDOW % tq == 0 and SEL_BLOCK == 64 and n_blocks * (SEL_BLOCK // CMP_STRIDE) == n_cmp
    wchunks = WINDOW // tq

    @pl.when(i == 0)
    def _():
        ck = jnp.zeros((n_cmp, gw), F32)
        cvt = jnp.zeros((gw, n_cmp), F32)
        for c in range(nchunk):
            ck = ck + _dot(poolt_ref[c], k_ref[c * tq:(c + 1) * tq, 0:gw])
            cvt = cvt + _dot(vt_ref[c, 0:gw, :], pool_ref[c])
        ck_scr[...] = jnp.concatenate([ck.astype(BF16), caug_ref[...]], axis=1)
        cvt_scr[...] = cvt.astype(BF16)

    crow = lax.broadcasted_iota(jnp.int32, (n_cmp, tq), 0)
    qlane = lax.broadcasted_iota(jnp.int32, (n_cmp, tq), 1)
    cmp_ok = crow * CMP_STRIDE + (CMP_BLOCK - 1) <= q0 + qlane
    cmp_bias = jnp.where(cmp_ok, 0.0, NEG_INF)
    qpos_b = q0 + lax.broadcasted_iota(jnp.int32, (n_blocks, tq), 1)
    tri = _causal_bias_t(tq)
    kk = lax.broadcasted_iota(jnp.int32, (tq, tq), 0)
    qq = lax.broadcasted_iota(jnp.int32, (tq, tq), 1)
    win_edge = jnp.where(kk > qq, 0.0, NEG_INF)
    zeros = jnp.zeros((HEAD_DIM, tq), BF16)

    def gate(br, h):
        r = H_B + br * H_C + h
        return jax.nn.sigmoid(gct_ref[r:r + 1, :])

    def put_q(h, block_bias):
        g = h // nh
        pos = PERM_C.index(h)
        qh = _scale_q(qt_ref[pos * HEAD_DIM:(pos + 1) * HEAD_DIM, :])
        aug = _alibi_rows(_alibi_slope(h, H_C), tq, block_bias)
        qa_scr[h] = jnp.concatenate([qh, zeros, aug] if g == 0 else [zeros, qh, aug], axis=0)

    for g in range(G_C):
        imp = jnp.zeros((n_cmp, tq), F32)
        for h in range(nh * g, nh * (g + 1)):
            put_q(h, None)
            st = _dot(ck_scr[...], qa_scr[h]) + cmp_bias
            p = jnp.exp(st - jnp.max(st, axis=0, keepdims=True))
            p = jnp.where(cmp_ok, p / jnp.sum(p, axis=0, keepdims=True), 0.0)
            imp = imp + p
            oc_scr[h] = gate(0, h) * _dot(cvt_scr[g * HEAD_DIM:(g + 1) * HEAD_DIM, :], p.astype(BF16))
        for t in range(tq // LANES):
            imp_scr[t] = imp[:, t * LANES:(t + 1) * LANES]
        bias = jnp.concatenate(
            [_select_blocks_t(imp_scr.at[t], qpos_b[:, t * LANES:(t + 1) * LANES], n_blocks) for t in range(tq // LANES)],
            axis=1)
        for h in range(nh * g, nh * (g + 1)):
            put_q(h, bias)

    def attend(kcol, vrow, kaug, steps_fn, br):
        _t_flash_init(m_scr, l_scr, acc_scr)

        def step(c, extra):
            rows = pl.ds(pl.multiple_of(c * tq, tq), tq)
            ka = jnp.concatenate([k_ref[rows, kcol * gw:(kcol + 1) * gw], kaug[c]], axis=1)
            for h in range(H_C):
                g = h // nh
                st = _dot(ka, qa_scr[h])
                if extra is not None:
                    st = st + extra
                _t_flash_update(st, h, m_scr, l_scr, acc_scr,
                                vt_ref[c, vrow * gw + g * HEAD_DIM:vrow * gw + (g + 1) * HEAD_DIM, :])

        steps_fn(step)
        for h in range(H_C):
            oc_scr[h] = oc_scr[h] + gate(br, h) * (acc_scr[h] / l_scr[h])

    def sel_steps(step):
        def body(c, carry):
            step(c, None)
            return carry
        lax.fori_loop(0, i, body, 0)
        step(i, tri)

    def win_steps(step):
        @pl.when(i >= wchunks)
        def _():
            step(i - wchunks, win_edge)
        for back in range(wchunks - 1, 0, -1):
            @pl.when(i >= back)
            def _(back=back):
                step(i - back, None)
        step(i, tri)

    attend(1, 1, ksel_ref, sel_steps, 1)
    attend(2, 2, kpos_ref, win_steps, 2)
    for h in range(H_C):
        pos = PERM_C.index(h)
        ot_scr[pos * HEAD_DIM:(pos + 1) * HEAD_DIM, :] = oc_scr[h]
    o_ref[...] = ot_scr[...].T.astype(BF16)


def _nsa_prompt_t(qt, kc, vct, fbgct, n_batch, seq):
    nq = seq // TQ
    n = kc.shape[0]
    gw = G_C * HEAD_DIM
    n_blocks = seq // SEL_BLOCK
    pool_np = _pool_matrix(seq, LANES)
    pool = jnp.asarray(pool_np.reshape(nq, TQ, LANES), BF16)
    poolt = jnp.asarray(pool_np.reshape(nq, TQ, LANES).transpose(0, 2, 1), BF16)
    caug = jnp.asarray(_pos_aug(np.arange(LANES) * CMP_STRIDE + CMP_BLOCK - 1), BF16)
    kpos = jnp.asarray(_pos_aug(np.arange(seq)).reshape(nq, TQ, LANES), BF16)
    ksel = jnp.asarray(_pos_aug(np.arange(seq), n_blocks).reshape(nq, TQ, LANES), BF16)
    const3 = lambda b, i: (0, 0, 0)
    return pl.pallas_call(
        _nsa_prompt_t_body,
        grid=(n_batch, nq),
        in_specs=[
            pl.BlockSpec((None, None, W_C, TQ), lambda b, i: (b, i, 2, 0)),
            pl.BlockSpec((seq, 3 * gw), lambda b, i: (b, 0)),
            pl.BlockSpec((None, nq, 3 * gw, TQ), lambda b, i: (b, 0, 0, 0)),
            pl.BlockSpec((None, 4 * H_B, TQ), lambda b, i: (b, 0, i)),
            pl.BlockSpec((nq, TQ, LANES), const3),
            pl.BlockSpec((nq, LANES, TQ), const3),
            pl.BlockSpec((LANES, LANES), lambda b, i: (0, 0)),
            pl.BlockSpec((nq, TQ, LANES), const3),
            pl.BlockSpec((nq, TQ, LANES), const3),
        ],
        out_specs=pl.BlockSpec((TQ, W_C), lambda b, i: (b * nq + i, 0)),
        out_shape=jax.ShapeDtypeStruct((n, W_C), BF16),
        scratch_shapes=[
            pltpu.VMEM((LANES, 2 * LANES), BF16), pltpu.VMEM((gw, LANES), BF16),
            pltpu.VMEM((TQ // LANES, LANES, LANES), F32), pltpu.VMEM((H_C, 2 * LANES, TQ), BF16),
            pltpu.VMEM((H_C, 1, TQ), F32), pltpu.VMEM((H_C, 1, TQ), F32), pltpu.VMEM((H_C, HEAD_DIM, TQ), F32),
            pltpu.VMEM((H_C, HEAD_DIM, TQ), F32), pltpu.VMEM((W_C, TQ), F32),
        ],
        compiler_params=_params("arbitrary", "arbitrary"), name="nsa_prompt",
    )(qt, kc, vct, fbgct, pool, poolt, caug, kpos, ksel)


def _pad_rows(x, n):
    return jnp.concatenate([x, jnp.zeros((n - x.shape[0], x.shape[1]), x.dtype)], axis=0).astype(BF16)


def _query_index(shape, t):
    assert t & (t - 1) == 0
    return lax.broadcasted_iota(jnp.int32, shape, 0) & (t - 1)


def _two_piece_softmax(s_past, s_new):
    m = jnp.maximum(jnp.max(s_past, axis=1, keepdims=True), jnp.max(s_new, axis=1, keepdims=True))
    p_past = jnp.exp(s_past - m)
    p_new = jnp.exp(s_new - m)
    l = jnp.sum(p_past, axis=1, keepdims=True) + jnp.sum(p_new, axis=1, keepdims=True)
    return p_past / l, p_new / l


def _diff_decode_body(pt_ref, lam_ref, gh_ref, q_ref, kvn_ref, *rest, lam_init, n_pages):
    pages = rest[:n_pages]
    o_ref = rest[n_pages]
    t = q_ref.shape[0]
    past = n_pages * PAGE
    lam = _diff_lambda(lam_ref, lam_init)
    ti = _query_index((2 * t, LANES), t)
    lane = lax.broadcasted_iota(jnp.int32, (2 * t, LANES), 1)
    new_bias = jnp.where(lane <= ti, 0.0, NEG_INF)
    kpos_past = (lax.broadcasted_iota(jnp.int32, (1, past), 1) - past).astype(F32)
    kpos_new = lane.astype(F32)
    for h in range(H_A):
        cols = slice(h * DA, (h + 1) * DA)
        q1, q2 = _scaled_halves(q_ref[:, cols])
        qq = jnp.concatenate([q1, q2], axis=0)
        slope = _alibi_slope(h, H_A)
        s_past = jnp.concatenate(
            [_dot_nt(qq, pg[pl.ds(h, PAGE, stride=2 * H_A), :].astype(BF16)) for pg in pages], axis=1)
        s_past = s_past + slope * kpos_past
        k_new = _pad_rows(kvn_ref[:, cols], PAGE)
        v_new = _pad_rows(kvn_ref[:, W_A + h * DA:W_A + (h + 1) * DA], PAGE)
        s_new = _dot_nt(qq, k_new) + slope * kpos_new + new_bias
        p_past, p_new = _two_piece_softmax(s_past, s_new)
        a_past = (p_past[0:t] - lam * p_past[t:2 * t]).astype(BF16)
        a_new = (p_new[0:t] - lam * p_new[t:2 * t]).astype(BF16)
        o = _dot(a_new, v_new)
        for p, pg in enumerate(pages):
            o = o + _dot(a_past[:, p * PAGE:(p + 1) * PAGE], pg[pl.ds(H_A + h, PAGE, stride=2 * H_A), :].astype(BF16))
        o_ref[:, cols] = _head_norm(o, gh_ref[h:h + 1, :], lam_init)


def _page_specs(block, layer, n_pages):
    return [pl.BlockSpec((None, None) + block, functools.partial(lambda p, b, pt: (layer, pt[b, p], 0, 0), p))
            for p in range(n_pages)]


def _diff_decode(page_table, lam_vec, g_head, q_all, dkvb, cache, layer, t_dec, lam_init):
    n_req, n_pages = page_table.shape
    rows = PAGE * 2 * H_A
    grid_spec = pltpu.PrefetchScalarGridSpec(
        num_scalar_prefetch=1, grid=(n_req,),
        in_specs=[
            pl.BlockSpec((4, HEAD_DIM), lambda b, pt: (0, 0)),
            pl.BlockSpec((H_A, DA), lambda b, pt: (0, 0)),
            pl.BlockSpec((t_dec, W_A), lambda b, pt: (b, 0)),
            pl.BlockSpec((t_dec, 2 * W_A), lambda b, pt: (b, 0)),
        ] + _page_specs((rows, DA), layer, n_pages),
        out_specs=pl.BlockSpec((t_dec, W_A), lambda b, pt: (b, 0)),
    )
    return pl.pallas_call(
        functools.partial(_diff_decode_body, lam_init=lam_init, n_pages=n_pages),
        grid_spec=grid_spec, out_shape=jax.ShapeDtypeStruct((n_req * t_dec, W_A), F32),
        compiler_params=_params("arbitrary"), name="diff_decode",
    )(page_table, lam_vec, g_head, q_all, dkvb, *([cache] * n_pages))


def _fox_decode_body(pt_ref, q_ref, kvn_ref, lfn_ref, *rest, n_pages):
    pages = rest[:n_pages]
    lf_pages = rest[n_pages:2 * n_pages]
    o_ref = rest[2 * n_pages]
    t = q_ref.shape[0]
    past = n_pages * PAGE
    nh = 4
    d_all = _lane_cumsum(jnp.concatenate([lf[...] for lf in lf_pages] + [lfn_ref[...]], axis=1))
    ti = _query_index((nh * t, LANES), t)
    lane = lax.broadcasted_iota(jnp.int32, (nh * t, LANES), 1)
    new_bias = jnp.where(lane <= ti, 0.0, NEG_INF)
    lo, _ = _half_masks((t, LANES))
    for gp in range(KV_B // 2):
        heads = _pair_heads_b(gp)
        halves = [_scaled_halves(q_ref[:, tl * LANES:(tl + 1) * LANES]) for tl in (2 * gp, 2 * gp + 1)]
        qq = jnp.concatenate([halves[tl - 2 * gp][half] for tl, half, _ in heads], axis=0)
        d_rows = jnp.concatenate([jnp.broadcast_to(d_all[h:h + 1, :], (t, past + LANES)) for _, _, h in heads], axis=0)
        krows = slice(gp * LANES, (gp + 1) * LANES)
        vrows = slice(KV_B * HEAD_DIM + gp * LANES, KV_B * HEAD_DIM + (gp + 1) * LANES)
        s_past = jnp.concatenate([_dot(qq, pg[krows, :].astype(BF16)) for pg in pages], axis=1) - d_rows[:, :past]
        k_new = _pad_rows(kvn_ref[:, krows], PAGE)
        v_new = _pad_rows(kvn_ref[:, vrows], PAGE)
        s_new = _dot_nt(qq, k_new) - d_rows[:, past:] + new_bias
        p_past, p_new = _two_piece_softmax(s_past, s_new)
        p_past = p_past.astype(BF16)
        o = _dot(p_new.astype(BF16), v_new)
        for p, pg in enumerate(pages):
            o = o + _dot_nt(p_past[:, p * PAGE:(p + 1) * PAGE], pg[vrows, :].astype(BF16))
        o_ref[:, (2 * gp) * LANES:(2 * gp + 1) * LANES] = jnp.where(lo, o[0:t], o[t:2 * t])
        o_ref[:, (2 * gp + 1) * LANES:(2 * gp + 2) * LANES] = jnp.where(lo, o[2 * t:3 * t], o[3 * t:4 * t])


def _fox_decode(page_table, q_all, foxb, lfn, cache_kv, cache_lf, layer, t_dec):
    n_req, n_pages = page_table.shape
    grid_spec = pltpu.PrefetchScalarGridSpec(
        num_scalar_prefetch=1, grid=(n_req,),
        in_specs=[
            pl.BlockSpec((t_dec, W_B), lambda b, pt: (b, 1)),
            pl.BlockSpec((t_dec, 2 * KV_B * HEAD_DIM), lambda b, pt: (b, 0)),
            pl.BlockSpec((None, H_B, LANES), lambda b, pt: (b, 0, 0)),
        ] + _page_specs((2 * KV_B * HEAD_DIM, PAGE), layer, n_pages) + _page_specs((H_B, PAGE), layer, n_pages),
        out_specs=pl.BlockSpec((t_dec, W_B), lambda b, pt: (b, 0)),
    )
    return pl.pallas_call(
        functools.partial(_fox_decode_body, n_pages=n_pages),
        grid_spec=grid_spec, out_shape=jax.ShapeDtypeStruct((n_req * t_dec, W_B), F32),
        compiler_params=_params("arbitrary"), name="fox_decode",
    )(page_table, q_all, foxb, lfn, *([cache_kv] * n_pages), *([cache_lf] * n_pages))


def _nsa_decode_body(pt_ref, q_ref, kvn_ref, gc_ref, win_ref, pool_ref, *rest, n_pages):
    pages = rest[:n_pages]
    o_ref = rest[n_pages]
    t = q_ref.shape[0]
    past = n_pages * PAGE
    wbuf = win_ref.shape[1]
    nh = H_C // G_C
    n_blocks = (past + SEL_BLOCK) // SEL_BLOCK
    n_sel_lanes = 2 * LANES
    ti = _query_index((nh * t, LANES), t)
    lane = lax.broadcasted_iota(jnp.int32, (nh * t, LANES), 1)
    tq = past + ti

    ck = jnp.zeros((LANES, LANES), F32)
    cv = jnp.zeros((LANES, LANES), F32)
    for p, pg in enumerate(pages):
        ck = ck + _dot(pg[0:LANES, :].astype(BF16), pool_ref[p])
        cv = cv + _dot(pg[LANES:2 * LANES, :].astype(BF16), pool_ref[p])
    ckt = ck.astype(BF16)
    cvt = cv.astype(BF16)
    cpos = lane * CMP_STRIDE + (CMP_BLOCK - 1)
    cmp_ok = cpos <= tq
    cmp_bias = jnp.where(cmp_ok, 0.0, NEG_INF)
    cpos_rel = (cpos - past).astype(F32)

    kpos_past = lax.broadcasted_iota(jnp.int32, (1, past), 1)
    kpos_past_rel = (kpos_past - past).astype(F32)
    new_ok = lane <= ti
    kpos_new_rel = lane.astype(F32)
    wpos = past - wbuf + lax.broadcasted_iota(jnp.int32, (nh * t, wbuf), 1)
    tq_w = past + _query_index((nh * t, wbuf), t)
    win_bias = jnp.where((wpos <= tq_w) & (tq_w - wpos < WINDOW) & (wpos >= 0), 0.0, NEG_INF)
    wpos_rel = (wpos - past).astype(F32)
    win_new_bias = jnp.where(new_ok & (ti - lane < WINDOW), 0.0, NEG_INF)

    def gate_col(br, g):
        cols = [jax.nn.sigmoid(gc_ref[:, GC_LANE0 + br * H_C + nh * g + r:GC_LANE0 + br * H_C + nh * g + r + 1])
                for r in range(nh)]
        return jnp.concatenate(cols, axis=0)

    lo, hi = _half_masks((t, LANES))
    outs = []
    for g in range(G_C):
        qq = jnp.concatenate([_scaled_halves(q_ref[:, r * LANES:(r + 1) * LANES])[g] for r in range(nh)], axis=0)
        head_row = lax.broadcasted_iota(jnp.int32, (nh * t, 1), 0)
        slope = jnp.full((nh * t, 1), _alibi_slope(nh * g, H_C), F32)
        for r in range(1, nh):
            slope = jnp.where(head_row >= r * t, _alibi_slope(nh * g + r, H_C), slope)

        s = _dot(qq, ckt) + slope * cpos_rel + cmp_bias
        p = jnp.exp(s - jnp.max(s, axis=1, keepdims=True))
        p = jnp.where(cmp_ok, p / jnp.sum(p, axis=1, keepdims=True), 0.0)
        o = gate_col(0, g) * _dot_nt(p.astype(BF16), cvt)
        imp = p[0:t]
        for r in range(1, nh):
            imp = imp + p[r * t:(r + 1) * t]
        imp = jnp.concatenate([imp, jnp.zeros((t, n_sel_lanes - LANES), F32)], axis=1)

        qpos = past + lax.broadcasted_iota(jnp.int32, (t, n_sel_lanes), 0)
        score, valid, blk = _block_scores(imp, qpos, n_blocks)
        sel = jnp.where(_select_blocks(score, valid, blk, n_blocks), 1.0, 0.0)
        lo_t, _ = _half_masks((t, LANES))
        tiles = []
        for c in range(n_pages + 1):
            a = sel[:, 4 * (2 * c) + 3:4 * (2 * c) + 4]
            b = sel[:, 4 * (2 * c + 1) + 3:4 * (2 * c + 1) + 4] if 2 * c + 1 < n_blocks else jnp.zeros((t, 1), F32)
            tiles.append(jnp.where(lo_t, a, b))
        sel_keys = jnp.concatenate(tiles, axis=1)
        sel_keys = jnp.concatenate([sel_keys] * nh, axis=0)
        sel_bias_past = jnp.where(sel_keys[:, :past] > 0.5, 0.0, NEG_INF)
        sel_bias_new = jnp.where((sel_keys[:, past:] > 0.5) & new_ok, 0.0, NEG_INF)

        s_past = jnp.concatenate([_dot(qq, pg[2 * LANES:3 * LANES, :].astype(BF16)) for pg in pages], axis=1)
        s_past = s_past + slope * kpos_past_rel + sel_bias_past
        k_new = _pad_rows(kvn_ref[:, 2 * LANES:3 * LANES], PAGE)
        v_new = _pad_rows(kvn_ref[:, 3 * LANES:4 * LANES], PAGE)
        s_new = _dot_nt(qq, k_new) + slope * kpos_new_rel + sel_bias_new
        p_past, p_new = _two_piece_softmax(s_past, s_new)
        p_past = p_past.astype(BF16)
        osel = _dot(p_new.astype(BF16), v_new)
        for pi, pg in enumerate(pages):
            osel = osel + _dot_nt(p_past[:, pi * PAGE:(pi + 1) * PAGE], pg[3 * LANES:4 * LANES, :].astype(BF16))
        o = o + gate_col(1, g) * osel

        s_past = _dot(qq, win_ref[0:LANES, :].astype(BF16)) + slope * wpos_rel + win_bias
        k_new = _pad_rows(kvn_ref[:, 4 * LANES:5 * LANES], PAGE)
        v_new = _pad_rows(kvn_ref[:, 5 * LANES:6 * LANES], PAGE)
        s_new = _dot_nt(qq, k_new) + slope * kpos_new_rel + win_new_bias
        p_past, p_new = _two_piece_softmax(s_past, s_new)
        owin = _dot(p_new.astype(BF16), v_new) + _dot_nt(p_past.astype(BF16), win_ref[LANES:2 * LANES, :].astype(BF16))
        o = o + gate_col(2, g) * owin
        outs.append(o)

    for r in range(nh):
        o_ref[:, r * LANES:(r + 1) * LANES] = jnp.where(lo, outs[0][r * t:(r + 1) * t], outs[1][r * t:(r + 1) * t])


def _nsa_decode(page_table, q_all, nsab, fbgc, win_state, cache, layer, t_dec):
    n_req, n_pages = page_table.shape
    past = n_pages * PAGE
    wbuf = win_state.shape[-1]
    pool = jnp.asarray(_pool_matrix(past, LANES).reshape(n_pages, PAGE, LANES), BF16)
    grid_spec = pltpu.PrefetchScalarGridSpec(
        num_scalar_prefetch=1, grid=(n_req,),
        in_specs=[
            pl.BlockSpec((t_dec, W_C), lambda b, pt: (b, 2)),
            pl.BlockSpec((t_dec, 6 * G_C * HEAD_DIM), lambda b, pt: (b, 0)),
            pl.BlockSpec((t_dec, LANES), lambda b, pt: (b, 0)),
            pl.BlockSpec((None, None, 2 * G_C * HEAD_DIM, wbuf), lambda b, pt: (layer, b, 0, 0)),
            pl.BlockSpec((n_pages, PAGE, LANES), lambda b, pt: (0, 0, 0)),
        ] + _page_specs((4 * G_C * HEAD_DIM, PAGE), layer, n_pages),
        out_specs=pl.BlockSpec((t_dec, W_C), lambda b, pt: (b, 0)),
    )
    return pl.pallas_call(
        functools.partial(_nsa_decode_body, n_pages=n_pages),
        grid_spec=grid_spec, out_shape=jax.ShapeDtypeStruct((n_req * t_dec, W_C), F32),
        compiler_params=_params("arbitrary"), name="nsa_decode",
    )(page_table, q_all, nsab, fbgc, win_state, pool, *([cache] * n_pages))


def _finish_body(x_ref, oa_ref, ob_ref, oc_ref, gate_ref, wa_ref, wb_ref, wc_ref, wo_ref, g_ref, y_ref):
    m = gate_ref[:, 0:D_MODEL].astype(F32) * _dot(oa_ref[...].astype(BF16), wa_ref[...])
    m = m + gate_ref[:, D_MODEL:2 * D_MODEL].astype(F32) * _dot(ob_ref[...].astype(BF16), wb_ref[...])
    m = m + gate_ref[:, 2 * D_MODEL:3 * D_MODEL].astype(F32) * _dot(oc_ref[...].astype(BF16), wc_ref[...])
    y = _dot(m.astype(BF16), wo_ref[...])
    y_ref[...] = x_ref[...] + _rms(y, g_ref[...])


def _finish(x, oa, ob, oc, gate, wa, wb, wc, wo, g):
    n = x.shape[0]
    tm = min(TQ, n)
    row = lambda i: (i, 0)
    const = lambda i: (0, 0)
    return pl.pallas_call(
        _finish_body, grid=(n // tm,),
        in_specs=[
            pl.BlockSpec((tm, D_MODEL), row), pl.BlockSpec((tm, W_A), row), pl.BlockSpec((tm, W_B), row),
            pl.BlockSpec((tm, W_C), row), pl.BlockSpec((tm, 3 * D_MODEL), row),
            _resident((W_A, D_MODEL), const), _resident((W_B, D_MODEL), const), _resident((W_C, D_MODEL), const),
            _resident((D_MODEL, D_MODEL), const), pl.BlockSpec((1, D_MODEL), const),
        ],
        out_specs=pl.BlockSpec((tm, D_MODEL), row),
        out_shape=jax.ShapeDtypeStruct((n, D_MODEL), F32),
        compiler_params=_params("arbitrary"), name="finish",
    )(x, oa, ob, oc, gate, wa, wb, wc, wo, g)


def _mlp_body(x_ref, g2_ref, g3_ref, wu_ref, wd_ref, y_ref):
    x = x_ref[...]
    h = _rms(x, g2_ref[...]).astype(BF16)
    fc = D_MODEL
    d = jnp.zeros(x.shape, F32)
    for c in range(D_FF // fc):
        u = jnp.maximum(_dot(h, wu_ref[:, c * fc:(c + 1) * fc]), 0.0)
        d = d + _dot((u * u).astype(BF16), wd_ref[c * fc:(c + 1) * fc, :])
    y_ref[...] = x + _rms(d, g3_ref[...])


def _mlp(x, g2, g3, wu, wd):
    n = x.shape[0]
    tm = min(TQ, n)
    row = lambda i: (i, 0)
    const = lambda i: (0, 0)
    return pl.pallas_call(
        _mlp_body, grid=(n // tm,),
        in_specs=[
            pl.BlockSpec((tm, D_MODEL), row), pl.BlockSpec((1, D_MODEL), const), pl.BlockSpec((1, D_MODEL), const),
            _resident((D_MODEL, D_FF), const), _resident((D_FF, D_MODEL), const),
        ],
        out_specs=pl.BlockSpec((tm, D_MODEL), row),
        out_shape=jax.ShapeDtypeStruct((n, D_MODEL), F32),
        compiler_params=_params("arbitrary"), name="mlp",
    )(x, g2, g3, wu, wd)


def _prep_w_in(w_in):
    wt = jnp.transpose(w_in, (0, 2, 1))

    def rows(lo, n):
        return wt[:, lo:lo + n, :].astype(BF16)

    pieces = [rows(_OFF_KA, 2 * W_A), rows(_OFF_KB, 2 * KV_B * HEAD_DIM), rows(_OFF_KVC, 6 * G_C * HEAD_DIM),
              rows(_OFF_FB, H_B), rows(_OFF_GC, 3 * H_C),
              jnp.zeros((wt.shape[0], LANES - H_B - 3 * H_C, D_MODEL), BF16),
              rows(_OFF_QA, W_A)]
    pieces += [rows(_OFF_QB + h * HEAD_DIM, HEAD_DIM) for h in PERM_B]
    pieces += [rows(_OFF_QC + h * HEAD_DIM, HEAD_DIM) for h in PERM_C]
    pieces.append(rows(_OFF_GM, 3 * D_MODEL))
    out = jnp.concatenate(pieces, axis=1)
    assert out.shape[1] == NW
    return out


def _perm_rows(w, perm):
    return jnp.concatenate([w[:, h * HEAD_DIM:(h + 1) * HEAD_DIM, :] for h in perm], axis=1).astype(BF16)


def kernel(x_prompt, x_sample, cache_diff_kv, cache_fox_kv, cache_fox_logf, cache_nsa_kv, state_nsa_win_kv, page_table, w_in, b_f, diff_lam, diff_norm_g, w_branch_a, w_branch_b, w_branch_c, w_out, norm_g, w_up, w_down):
    n_batch, seq, _ = x_prompt.shape
    n_req, t_dec, _ = x_sample.shape
    depth = w_in.shape[0]
    n_phys = cache_diff_kv.shape[1]
    n_pages = page_table.shape[1]
    assert seq % TQ == 0

    wt_all = _prep_w_in(w_in)
    wa_all = w_branch_a.astype(BF16)
    wb_all = _perm_rows(w_branch_b, PERM_B)
    wc_all = _perm_rows(w_branch_c, PERM_C)
    wo_all = w_out.astype(BF16)
    wu_all = w_up.astype(BF16)
    wd_all = w_down.astype(BF16)

    c_diff = cache_diff_kv.reshape(depth, n_phys, PAGE * 2 * H_A, DA)
    c_fox = jnp.transpose(cache_fox_kv, (0, 1, 3, 4, 5, 2)).reshape(depth, n_phys, 2 * KV_B * HEAD_DIM, PAGE)
    c_lf = jnp.transpose(cache_fox_logf, (0, 1, 3, 2))
    c_nsa = jnp.transpose(cache_nsa_kv, (0, 1, 3, 4, 5, 2)).reshape(depth, n_phys, 4 * G_C * HEAD_DIM, PAGE)
    wbuf = state_nsa_win_kv.shape[2]
    c_win = jnp.transpose(state_nsa_win_kv, (0, 1, 3, 4, 5, 2)).reshape(depth, n_req, 2 * G_C * HEAD_DIM, wbuf)

    xp = x_prompt.reshape(n_batch * seq, D_MODEL)
    xs = x_sample.reshape(n_req * t_dec, D_MODEL)
    outs_p = [[] for _ in range(5)]
    outs_s = [[] for _ in range(5)]
    n_win = min(WINDOW, seq)
    for l in range(depth):
        lam_init = 0.8 - 0.6 * float(np.exp(-0.3 * l))
        g = norm_g[l]
        bf_col = b_f[l].reshape(H_B, 1)
        bf_row = jnp.pad(b_f[l], (0, LANES - H_B)).reshape(1, LANES)
        wt = wt_all[l]

        dkv, ka, vat, qat, foxt, foxtb, nsat, wint, nsatb, logft, gc, q_bc, gate = _inproj_prompt(
            xp, g[0:1], bf_col, wt, n_batch, seq)
        oa = _diff_prompt_t(diff_lam[l], diff_norm_g[l], qat, ka, vat, n_batch, seq, lam_init)
        ob = _fox_prompt(q_bc, foxtb, logft, n_batch, seq)
        oc = _nsa_prompt(q_bc, nsatb, gc, n_batch, seq)
        xp = _finish(xp, oa, ob, oc, gate, wa_all[l], wb_all[l], wc_all[l], wo_all[l], g[1:2])
        xp = _mlp(xp, g[2:3], g[3:4], wu_all[l], wd_all[l])
        for lst, s in zip(outs_p, (dkv, foxt, logft, nsat, wint[:, :, seq - n_win:])):
            lst.append(s)

        dkv, fox, nsa, logft_s, fbgc, q_all, gate = _inproj_sample(xs, g[0:1], bf_col, bf_row, wt)
        lfn = jnp.pad(jnp.transpose(logft_s.reshape(H_B, n_req, t_dec), (1, 0, 2)), ((0, 0), (0, 0), (0, LANES - t_dec)))
        oa = _diff_decode(page_table, diff_lam[l], diff_norm_g[l], q_all, dkv, c_diff, l, t_dec, lam_init)
        ob = _fox_decode(page_table, q_all, fox, lfn, c_fox, c_lf, l, t_dec)
        oc = _nsa_decode(page_table, q_all, nsa, fbgc, c_win, c_nsa, l, t_dec)
        xs = _finish(xs, oa, ob, oc, gate, wa_all[l], wb_all[l], wc_all[l], wo_all[l], g[1:2])
        xs = _mlp(xs, g[2:3], g[3:4], wu_all[l], wd_all[l])
        for lst, s in zip(outs_s, (dkv, fox, fbgc[:, 0:H_B], nsa[:, 0:4 * G_C * HEAD_DIM], nsa[:, 4 * G_C * HEAD_DIM:])):
            lst.append(s)

    def tr(stack, dims):
        a = jnp.stack(stack)
        a = a.reshape(a.shape[:2] + dims + a.shape[3:])
        return jnp.moveaxis(a, -1, 2)

    dkv_p = jnp.stack(outs_p[0]).reshape(depth, n_batch, seq, 2, H_A, DA)
    fkv_p = tr(outs_p[1], (2, KV_B, HEAD_DIM))
    flf_p = tr(outs_p[2], (H_B,))
    nkv_p = tr(outs_p[3], (4, G_C, HEAD_DIM))
    nwin_p = tr(outs_p[4], (2, G_C, HEAD_DIM))
    dkv_s = jnp.stack(outs_s[0]).reshape(depth, n_req, t_dec, 2, H_A, DA)
    fkv_s = jnp.stack(outs_s[1]).reshape(depth, n_req, t_dec, 2, KV_B, HEAD_DIM)
    flf_s = jnp.stack(outs_s[2]).reshape(depth, n_req, t_dec, H_B)
    nkv_s = jnp.stack(outs_s[3]).reshape(depth, n_req, t_dec, 4, G_C, HEAD_DIM)
    nwin_s = jnp.stack(outs_s[4]).reshape(depth, n_req, t_dec, 2, G_C, HEAD_DIM)
    yp = xp.reshape(n_batch, seq, D_MODEL)
    ys = xs.reshape(n_req, t_dec, D_MODEL)
    return (yp, ys, dkv_p, dkv_s, fkv_p, fkv_s, flf_p, flf_s, nkv_p, nkv_s, nwin_p, nwin_s)
```

```python
import functools

import numpy as np
import jax
import jax.numpy as jnp
from jax import lax
from jax.experimental import pallas as pl
from jax.experimental.pallas import tpu as pltpu

F32 = jnp.float32
BF16 = jnp.bfloat16

D_MODEL = 1024
HEAD_DIM = 64
H_A = 4
DA = 2 * HEAD_DIM
W_A = H_A * DA
H_B = 8
KV_B = 4
W_B = H_B * HEAD_DIM
H_C = 8
G_C = 2
W_C = H_C * HEAD_DIM
CMP_STRIDE = 16
CMP_BLOCK = 2 * CMP_STRIDE
SEL_BLOCK = 64
N_SEL = 16
WINDOW = 512
D_FF = 4 * D_MODEL
PAGE = 128
RMS_EPS = 1e-6
NEG_INF = -1e30
FORCE_BONUS = 1e4
QK_SCALE = HEAD_DIM ** -0.5

LANES = 128
HALF = LANES // 2
TQ = 256
VMEM_LIMIT = 56 * 1024 * 1024

_OFF_QA, _OFF_KA, _OFF_VA = 0, W_A, 2 * W_A
_OFF_QB = 3 * W_A
_OFF_KB = _OFF_QB + W_B
_OFF_VB = _OFF_KB + KV_B * HEAD_DIM
_OFF_FB = _OFF_VB + KV_B * HEAD_DIM
_OFF_QC = _OFF_FB + H_B
_OFF_KVC = _OFF_QC + W_C
_OFF_GC = _OFF_KVC + 6 * G_C * HEAD_DIM
_OFF_GM = _OFF_GC + 3 * H_C
N_IN = _OFF_GM + 3 * D_MODEL

PERM_B = (0, 2, 1, 3, 4, 6, 5, 7)
PERM_C = (0, 4, 1, 5, 2, 6, 3, 7)

R_DIFF = 0
R_FOX = R_DIFF + 2 * W_A
R_NSA = R_FOX + 2 * KV_B * HEAD_DIM
R_FBGC = R_NSA + 6 * G_C * HEAD_DIM
R_Q = R_FBGC + LANES
R_GM = R_Q + W_A + W_B + W_C
NW = R_GM + 3 * D_MODEL
GC_LANE0 = H_B

NT_DIMS = (((1,), (1,)), ((), ()))


def _dot(a, b):
    return jnp.dot(a, b, preferred_element_type=F32)


def _dot_nt(a, b):
    return lax.dot_general(a, b, NT_DIMS, preferred_element_type=F32)


def _rms(x, g):
    return x * lax.rsqrt(jnp.mean(x * x, axis=-1, keepdims=True) + RMS_EPS) * g


def _log_sigmoid(x):
    return jnp.minimum(x, 0.0) - jnp.log1p(jnp.exp(-jnp.abs(x)))


def _tile_lanes(m, n):
    reps = n // m.shape[1]
    return m if reps == 1 else jnp.concatenate([m] * reps, axis=1)


def _params(*sem):
    return pltpu.CompilerParams(dimension_semantics=sem, vmem_limit_bytes=VMEM_LIMIT)


def _resident(shape, imap):
    return pl.BlockSpec(shape, imap, pipeline_mode=pl.Buffered(1))


def _inproj_prompt_body(x_ref, g_ref, bf_ref, w_ref, dkv_ref, ka_ref, vat_ref, qat_ref, foxt_ref, foxtb_ref,
                        nsat_ref, wint_ref, nsatb_ref, logft_ref, gc_ref, q_ref, gate_ref):
    h = _rms(x_ref[...], g_ref[...]).astype(BF16)

    def nn(lo, n):
        return _dot_nt(h, w_ref[lo:lo + n, :])

    def tt(lo, n):
        return _dot_nt(w_ref[lo:lo + n, :], h)

    z = nn(R_DIFF, 2 * W_A)
    for c in range(2 * H_A):
        dkv_ref[pl.ds(c, z.shape[0], stride=2 * H_A), :] = z[:, c * DA:(c + 1) * DA]
    ka_ref[...] = z[:, 0:W_A].astype(BF16)
    vat_ref[...] = tt(R_DIFF + W_A, W_A).astype(BF16)
    qat_ref[...] = tt(R_Q, W_A).astype(BF16)
    z = tt(R_FOX, 2 * KV_B * HEAD_DIM)
    foxt_ref[...] = z
    foxtb_ref[...] = z.astype(BF16)
    z = tt(R_NSA, 6 * G_C * HEAD_DIM)
    nsat_ref[...] = z[0:4 * G_C * HEAD_DIM]
    wint_ref[...] = z[4 * G_C * HEAD_DIM:]
    nsatb_ref[...] = z.astype(BF16)
    z = tt(R_FBGC, 16)
    logft_ref[...] = _log_sigmoid(z[0:H_B] + bf_ref[...])
    gc_ref[...] = nn(R_FBGC, LANES)
    q_ref[...] = nn(R_Q + W_A, W_B + W_C).astype(BF16)
    gate_ref[...] = jax.nn.sigmoid(nn(R_GM, 3 * D_MODEL)).astype(BF16)


def _inproj_sample_body(x_ref, g_ref, bf_ref, bfrow_ref, w_ref, dkv_ref, fox_ref, nsa_ref, logft_ref, fbgc_ref,
                        q_ref, gate_ref):
    h = _rms(x_ref[...], g_ref[...]).astype(BF16)

    def nn(lo, n):
        return _dot_nt(h, w_ref[lo:lo + n, :])

    dkv_ref[...] = nn(R_DIFF, 2 * W_A)
    fox_ref[...] = nn(R_FOX, 2 * KV_B * HEAD_DIM)
    nsa_ref[...] = nn(R_NSA, 6 * G_C * HEAD_DIM)
    zt = _dot_nt(w_ref[R_FBGC:R_FBGC + 16, :], h)
    logft_ref[...] = _log_sigmoid(zt[0:H_B] + bf_ref[...])
    z = nn(R_FBGC, LANES)
    lane = lax.broadcasted_iota(jnp.int32, z.shape, 1)
    fbgc_ref[...] = jnp.where(lane < H_B, _log_sigmoid(z + bfrow_ref[...]), z)
    q_ref[...] = nn(R_Q, W_A + W_B + W_C)
    gate_ref[...] = jax.nn.sigmoid(nn(R_GM, 3 * D_MODEL)).astype(BF16)


def _inproj_prompt(x, g, bf_col, wt, n_batch, seq):
    n = x.shape[0]
    nq = seq // TQ
    grid = (n_batch, nq)
    kvw = KV_B * HEAD_DIM
    gw = G_C * HEAD_DIM
    row = lambda b, i: (b * nq + i, 0)
    const2 = lambda b, i: (0, 0)
    tr = lambda b, i: (b, 0, i)
    chunk = lambda b, i: (b, i, 0, 0)
    outs = (
        ((n * 2 * H_A, DA), F32, (TQ * 2 * H_A, DA), row),
        ((n, W_A), BF16, (TQ, W_A), row),
        ((n_batch, nq, W_A, TQ), BF16, (None, None, W_A, TQ), chunk),
        ((n_batch, nq, W_A, TQ), BF16, (None, None, W_A, TQ), chunk),
        ((n_batch, 2 * kvw, seq), F32, (None, 2 * kvw, TQ), tr),
        ((n_batch, nq, 2 * kvw, TQ), BF16, (None, None, 2 * kvw, TQ), chunk),
        ((n_batch, 4 * gw, seq), F32, (None, 4 * gw, TQ), tr),
        ((n_batch, 2 * gw, seq), F32, (None, 2 * gw, TQ), tr),
        ((n_batch, nq, 6 * gw, TQ), BF16, (None, None, 6 * gw, TQ), chunk),
        ((n_batch, H_B, seq), F32, (None, H_B, TQ), tr),
        ((n, LANES), F32, (TQ, LANES), row),
        ((n, W_B + W_C), BF16, (TQ, W_B + W_C), row),
        ((n, 3 * D_MODEL), BF16, (TQ, 3 * D_MODEL), row),
    )
    in_specs = [
        pl.BlockSpec((TQ, D_MODEL), row),
        pl.BlockSpec((1, D_MODEL), const2),
        pl.BlockSpec((H_B, 1), const2),
        _resident((NW, D_MODEL), const2),
    ]
    return pl.pallas_call(
        _inproj_prompt_body, grid=grid, in_specs=in_specs,
        out_specs=tuple(pl.BlockSpec(blk, imap) for _, _, blk, imap in outs),
        out_shape=tuple(jax.ShapeDtypeStruct(shp, dt) for shp, dt, _, _ in outs),
        compiler_params=_params("arbitrary", "arbitrary"), name="inproj_prompt",
    )(x, g, bf_col, wt)


def _inproj_sample(x, g, bf_col, bf_row, wt):
    n = x.shape[0]
    tm = min(TQ, n)
    grid = (n // tm,)
    row = lambda i: (i, 0)
    const2 = lambda i: (0, 0)
    widths = ((2 * W_A, F32), (512, F32), (768, F32))
    out_shape = tuple(jax.ShapeDtypeStruct((n, w), dt) for w, dt in widths) + (
        jax.ShapeDtypeStruct((H_B, n), F32),
        jax.ShapeDtypeStruct((n, LANES), F32),
        jax.ShapeDtypeStruct((n, W_A + W_B + W_C), F32),
        jax.ShapeDtypeStruct((n, 3 * D_MODEL), BF16),
    )
    out_specs = tuple(pl.BlockSpec((tm, w), row) for w, _ in widths) + (
        pl.BlockSpec((H_B, tm), lambda i: (0, i)),
        pl.BlockSpec((tm, LANES), row),
        pl.BlockSpec((tm, W_A + W_B + W_C), row),
        pl.BlockSpec((tm, 3 * D_MODEL), row),
    )
    in_specs = [
        pl.BlockSpec((tm, D_MODEL), row),
        pl.BlockSpec((1, D_MODEL), const2),
        pl.BlockSpec((H_B, 1), const2),
        pl.BlockSpec((1, LANES), const2),
        _resident((NW, D_MODEL), const2),
    ]
    return pl.pallas_call(
        _inproj_sample_body, grid=grid, in_specs=in_specs, out_specs=out_specs, out_shape=out_shape,
        compiler_params=_params("arbitrary"), name="inproj_sample",
    )(x, g, bf_col, bf_row, wt)


def _flash_init(m_scr, l_scr, acc_scr):
    m_scr[...] = jnp.full(m_scr.shape, NEG_INF, F32)
    l_scr[...] = jnp.zeros(l_scr.shape, F32)
    acc_scr[...] = jnp.zeros(acc_scr.shape, F32)


def _flash_update(s, idx, m_scr, l_scr, acc_scr, pv):
    m_prev = m_scr[idx]
    m_next = jnp.maximum(m_prev, jnp.max(s, axis=1, keepdims=True))
    alpha = jnp.exp(m_prev - m_next)
    p = jnp.exp(s - _tile_lanes(m_next, s.shape[1]))
    l_scr[idx] = alpha * l_scr[idx] + jnp.sum(p, axis=1, keepdims=True)
    m_scr[idx] = m_next
    acc_scr[idx] = alpha * acc_scr[idx] + pv(p.astype(BF16))


def _causal_bias(n):
    r = lax.broadcasted_iota(jnp.int32, (n, n), 0)
    c = lax.broadcasted_iota(jnp.int32, (n, n), 1)
    return jnp.where(c <= r, 0.0, NEG_INF).astype(F32)


def _half_masks(shape):
    lane = lax.broadcasted_iota(jnp.int32, shape, 1)
    return lane < HALF, lane >= HALF


def _scaled_halves(q_tile):
    q = q_tile.astype(F32) * QK_SCALE
    lo, hi = _half_masks(q.shape)
    return jnp.where(lo, q, 0.0).astype(BF16), jnp.where(hi, q, 0.0).astype(BF16)


def _diff_lambda(lam_ref, lam_init):
    lv = lam_ref[...]
    a = jnp.sum(lv[0:1] * lv[1:2], axis=1, keepdims=True)
    b = jnp.sum(lv[2:3] * lv[3:4], axis=1, keepdims=True)
    return jnp.exp(a) - jnp.exp(b) + lam_init


def _head_norm(o, g_row, lam_init):
    return o * lax.rsqrt(jnp.mean(o * o, axis=-1, keepdims=True) + RMS_EPS) * g_row * (1.0 - lam_init)


def _alibi_slope(h, n_heads):
    return float(2.0 ** (-8.0 * (h + 1) / n_heads))


def _diff_prompt_body(lam_ref, gh_ref, q_ref, k_ref, v_ref, o_ref, m_scr, l_scr, acc_scr, *, lam_init):
    i = pl.program_id(1)
    tq = q_ref.shape[0]
    lam = _diff_lambda(lam_ref, lam_init)
    tri = _causal_bias(tq)
    kiota = lax.broadcasted_iota(jnp.int32, (1, tq), 1)
    for h in range(H_A):
        cols = slice(h * DA, (h + 1) * DA)
        q1, q2 = _scaled_halves(q_ref[:, cols])
        slope = _alibi_slope(h, H_A)
        _flash_init(m_scr, l_scr, acc_scr)

        def step(j, extra, q1=q1, q2=q2, slope=slope, cols=cols):
            rows = pl.ds(pl.multiple_of(j * tq, tq), tq)
            kc = k_ref[rows, cols]
            vc = v_ref[rows, cols]
            bias = slope * ((j - i) * tq + kiota).astype(F32)
            if extra is not None:
                bias = bias + extra
            for mi, qm in enumerate((q1, q2)):
                _flash_update(_dot_nt(qm, kc) + bias, mi, m_scr, l_scr, acc_scr, lambda p: _dot(p, vc))

        def body(j, c):
            step(j, None)
            return c

        lax.fori_loop(0, i, body, 0)
        step(i, tri)
        o = acc_scr[0] / l_scr[0] - lam * (acc_scr[1] / l_scr[1])
        o_ref[:, cols] = _head_norm(o, gh_ref[h:h + 1, :], lam_init).astype(BF16)


def _diff_prompt(lam_vec, g_head, q_all, kv_b, n_batch, seq, lam_init):
    nq = seq // TQ
    n = q_all.shape[0]
    return pl.pallas_call(
        functools.partial(_diff_prompt_body, lam_init=lam_init),
        grid=(n_batch, nq),
        in_specs=[
            pl.BlockSpec((4, HEAD_DIM), lambda b, i: (0, 0)),
            pl.BlockSpec((H_A, DA), lambda b, i: (0, 0)),
            pl.BlockSpec((TQ, W_A), lambda b, i: (b * nq + i, 0)),
            pl.BlockSpec((seq, W_A), lambda b, i: (b, 0)),
            pl.BlockSpec((seq, W_A), lambda b, i: (b, 1)),
        ],
        out_specs=pl.BlockSpec((TQ, W_A), lambda b, i: (b * nq + i, 0)),
        out_shape=jax.ShapeDtypeStruct((n, W_A), BF16),
        scratch_shapes=[pltpu.VMEM((2, TQ, LANES), F32)] * 3,
        compiler_params=_params("arbitrary", "arbitrary"), name="diff_prompt",
    )(lam_vec, g_head, q_all, kv_b, kv_b)


def _lane_cumsum(x):
    n = x.shape[1]
    lane = lax.broadcasted_iota(jnp.int32, x.shape, 1)
    sh = 1
    while sh < n:
        x = x + jnp.where(lane >= sh, pltpu.roll(x, sh, axis=1), 0.0)
        sh *= 2
    return x


def _pair_heads_b(gp):
    return ((2 * gp, 0, 4 * gp), (2 * gp, 1, 4 * gp + 2), (2 * gp + 1, 0, 4 * gp + 1), (2 * gp + 1, 1, 4 * gp + 3))


def _fox_prompt_body(q_ref, kvt_ref, lf_ref, o_ref, d_scr, m_scr, l_scr, acc_scr):
    i = pl.program_id(1)
    tq = q_ref.shape[0]
    nchunk = kvt_ref.shape[0]

    @pl.when(i == 0)
    def _():
        d = _lane_cumsum(lf_ref[...])
        for c in range(nchunk):
            d_scr[c] = d[:, c * tq:(c + 1) * tq]

    tri = _causal_bias(tq)
    lo, _ = _half_masks((tq, LANES))
    n_pairs = KV_B // 2
    qms = []
    for gp in range(n_pairs):
        halves = [_scaled_halves(q_ref[:, t * LANES:(t + 1) * LANES]) for t in (2 * gp, 2 * gp + 1)]
        qms.append([halves[t - 2 * gp][half] for t, half, _ in _pair_heads_b(gp)])
    _flash_init(m_scr, l_scr, acc_scr)

    def step(j, extra):
        dj = d_scr[j]
        for gp in range(n_pairs):
            ktc = kvt_ref[j, gp * LANES:(gp + 1) * LANES, :]
            vtc = kvt_ref[j, KV_B * HEAD_DIM + gp * LANES:KV_B * HEAD_DIM + (gp + 1) * LANES, :]
            for idx, (_, _, h) in enumerate(_pair_heads_b(gp)):
                s = _dot(qms[gp][idx], ktc) - dj[h:h + 1, :]
                if extra is not None:
                    s = s + extra
                _flash_update(s, 4 * gp + idx, m_scr, l_scr, acc_scr, lambda p, vtc=vtc: _dot_nt(p, vtc))

    def body(j, c):
        step(j, None)
        return c

    lax.fori_loop(0, i, body, 0)
    step(i, tri)
    for gp in range(n_pairs):
        outs = [acc_scr[4 * gp + idx] / l_scr[4 * gp + idx] for idx in range(4)]
        o_ref[:, (2 * gp) * LANES:(2 * gp + 1) * LANES] = jnp.where(lo, outs[0], outs[1]).astype(BF16)
        o_ref[:, (2 * gp + 1) * LANES:(2 * gp + 2) * LANES] = jnp.where(lo, outs[2], outs[3]).astype(BF16)


def _fox_prompt(q_all, foxtb, logft, n_batch, seq):
    nq = seq // TQ
    n = q_all.shape[0]
    return pl.pallas_call(
        _fox_prompt_body,
        grid=(n_batch, nq),
        in_specs=[
            pl.BlockSpec((TQ, W_B), lambda b, i: (b * nq + i, 0)),
            pl.BlockSpec((None, nq, 512, TQ), lambda b, i: (b, 0, 0, 0)),
            pl.BlockSpec((None, H_B, seq), lambda b, i: (b, 0, 0)),
        ],
        out_specs=pl.BlockSpec((TQ, W_B), lambda b, i: (b * nq + i, 0)),
        out_shape=jax.ShapeDtypeStruct((n, W_B), BF16),
        scratch_shapes=[pltpu.VMEM((nq, H_B, TQ), F32)] + [pltpu.VMEM((H_B, TQ, LANES), F32)] * 3,
        compiler_params=_params("arbitrary", "arbitrary"), name="fox_prompt",
    )(q_all, foxtb, logft)


def _block_scores(imp, qpos, n_blocks):
    lane = lax.broadcasted_iota(jnp.int32, imp.shape, 1)
    y = imp + pltpu.roll(imp, 1, axis=1)
    bs = y + pltpu.roll(y, 2, axis=1)
    blk = lane >> 2
    valid = ((lane & 3) == 3) & (blk < n_blocks) & (blk * SEL_BLOCK <= qpos)
    cur = qpos >> 6
    forced = (blk == 0) | (blk == cur) | (blk == cur - 1)
    score = jnp.where(valid, bs + FORCE_BONUS * jnp.where(forced, 1.0, 0.0), NEG_INF)
    return score, valid, blk


def _select_blocks(score, valid, blk, n_blocks):
    cnt = jnp.zeros(score.shape, F32)
    for jp in range(n_blocks):
        v = score[:, 4 * jp + 3:4 * jp + 4]
        better = (v > score) | ((v == score) & (blk > jp))
        cnt = cnt + jnp.where(better, 1.0, 0.0)
    return valid & (cnt < float(min(N_SEL, n_blocks)))


def _nsa_prompt_body(q_ref, kvt_ref, gc_ref, pool_ref, e_ref, o_ref, ck_scr, cv_scr, selb_scr, winb_scr,
                     m_scr, l_scr, acc_scr, oc_scr, imp_scr):
    i = pl.program_id(1)
    tq = q_ref.shape[0]
    nchunk = kvt_ref.shape[0]
    n_blocks = nchunk * tq // SEL_BLOCK
    q0 = i * tq
    lane_b = lax.broadcasted_iota(jnp.int32, (n_blocks, LANES), 1)

    @pl.when(i == 0)
    def _():
        ck = jnp.zeros((LANES, LANES), F32)
        cv = jnp.zeros((LANES, LANES), F32)
        for c in range(nchunk):
            ck = ck + _dot(kvt_ref[c, 0:LANES, :], pool_ref[c])
            cv = cv + _dot(kvt_ref[c, LANES:2 * LANES, :], pool_ref[c])
        ck_scr[...] = ck.astype(BF16)
        cv_scr[...] = cv.astype(BF16)

    row = lax.broadcasted_iota(jnp.int32, (tq, LANES), 0)
    lane = lax.broadcasted_iota(jnp.int32, (tq, LANES), 1)
    qpos = q0 + row
    cpos = lane * CMP_STRIDE + (CMP_BLOCK - 1)
    cmp_ok = cpos <= qpos
    cmp_bias = jnp.where(cmp_ok, 0.0, NEG_INF)
    cpos_rel = (cpos - q0).astype(F32)

    row_k = lax.broadcasted_iota(jnp.int32, (tq, tq), 0) + q0
    col_k = lax.broadcasted_iota(jnp.int32, (tq, tq), 1)
    kiota = lax.broadcasted_iota(jnp.int32, (1, tq), 1)
    n_win = WINDOW // tq + 1
    for slot in range(n_win):
        kpos = (i - (n_win - 1) + slot) * tq + col_k
        ok = (kpos <= row_k) & (row_k - kpos < WINDOW) & (kpos >= 0)
        winb_scr[slot] = jnp.where(ok, 0.0, NEG_INF)

    def gate(br, h):
        c = GC_LANE0 + br * H_C + h
        return jax.nn.sigmoid(gc_ref[:, c:c + 1])

    lo, hi = _half_masks((tq, LANES))
    nh = H_C // G_C
    qms = [_scaled_halves(q_ref[:, (h % nh) * LANES:(h % nh + 1) * LANES])[h // nh] for h in range(H_C)]
    slopes = [_alibi_slope(h, H_C) for h in range(H_C)]
    ckt = ck_scr[...]
    cvt = cv_scr[...]
    for g in range(G_C):
        imp = jnp.zeros((tq, LANES), F32)
        for h in range(nh * g, nh * (g + 1)):
            s = _dot(qms[h], ckt) + slopes[h] * cpos_rel + cmp_bias
            p = jnp.exp(s - jnp.max(s, axis=1, keepdims=True))
            p = jnp.where(cmp_ok, p / jnp.sum(p, axis=1, keepdims=True), 0.0)
            imp = imp + p
            oc_scr[h] = gate(0, h) * _dot_nt(p.astype(BF16), cvt)

        impt = imp.T
        for t in range(tq // LANES):
            imp_scr[t] = impt[:, t * LANES:(t + 1) * LANES]
        bias_t = jnp.concatenate(
            [_select_blocks_t(imp_scr, t, q0 + t * LANES + lane_b, n_blocks) for t in range(tq // LANES)],
            axis=1)
        sel_t = jnp.where(bias_t > 0.5 * NEG_INF, 1.0, 0.0)
        sel = jnp.concatenate([sel_t, jnp.zeros((LANES - n_blocks, tq), F32)], axis=0).T.astype(BF16)

        def mk_bias(c, carry, sel=sel, g=g):
            ex = _dot(sel, e_ref[c])
            ok = (ex > 0.5) & (c * tq + col_k <= row_k)
            selb_scr[g, c] = jnp.where(ok, 0.0, NEG_INF)
            return carry

        lax.fori_loop(0, i + 1, mk_bias, 0)

    _flash_init(m_scr, l_scr, acc_scr)

    def sel_step(c, carry):
        ktc = kvt_ref[c, 2 * LANES:3 * LANES, :]
        vtc = kvt_ref[c, 3 * LANES:4 * LANES, :]
        kpos_rel = ((c - i) * tq + kiota).astype(F32)
        for h in range(H_C):
            s = _dot(qms[h], ktc) + slopes[h] * kpos_rel + selb_scr[h // nh, c]
            _flash_update(s, h, m_scr, l_scr, acc_scr, lambda p: _dot_nt(p, vtc))
        return carry

    lax.fori_loop(0, i + 1, sel_step, 0)
    for h in range(H_C):
        oc_scr[h] = oc_scr[h] + gate(1, h) * (acc_scr[h] / l_scr[h])

    _flash_init(m_scr, l_scr, acc_scr)

    def win_step(slot, carry):
        c = i - (n_win - 1) + slot
        ktc = kvt_ref[c, 4 * LANES:5 * LANES, :]
        vtc = kvt_ref[c, 5 * LANES:6 * LANES, :]
        kpos_rel = ((c - i) * tq + kiota).astype(F32)
        mb = winb_scr[slot]
        for h in range(H_C):
            s = _dot(qms[h], ktc) + slopes[h] * kpos_rel + mb
            _flash_update(s, h, m_scr, l_scr, acc_scr, lambda p: _dot_nt(p, vtc))
        return carry

    lax.fori_loop(jnp.maximum(n_win - 1 - i, 0), n_win, win_step, 0)
    for h in range(H_C):
        oc_scr[h] = oc_scr[h] + gate(2, h) * (acc_scr[h] / l_scr[h])

    for t in range(H_C // G_C):
        o_ref[:, t * LANES:(t + 1) * LANES] = jnp.where(lo, oc_scr[t], oc_scr[t + H_C // G_C]).astype(BF16)


def _pool_matrix(length, n_cols):
    t = np.arange(length)[:, None]
    c = np.arange(n_cols)[None, :]
    m = (t >= c * CMP_STRIDE) & (t < c * CMP_STRIDE + CMP_BLOCK)
    return (m.astype(np.float32) / CMP_BLOCK)


def _expand_matrix(n_lanes, length):
    r = np.arange(n_lanes)[:, None]
    k = np.arange(length)[None, :]
    return (r == k // SEL_BLOCK).astype(np.float32)


def _nsa_prompt(q_all, nsatb, gc, n_batch, seq):
    nq = seq // TQ
    n = q_all.shape[0]
    pool = jnp.asarray(_pool_matrix(seq, LANES).reshape(nq, TQ, LANES), BF16)
    expand = jnp.asarray(_expand_matrix(LANES, seq).reshape(LANES, nq, TQ).transpose(1, 0, 2), BF16)
    n_win = WINDOW // TQ + 1
    return pl.pallas_call(
        _nsa_prompt_body,
        grid=(n_batch, nq),
        in_specs=[
            pl.BlockSpec((TQ, W_C), lambda b, i: (b * nq + i, 1)),
            pl.BlockSpec((None, nq, 768, TQ), lambda b, i: (b, 0, 0, 0)),
            pl.BlockSpec((TQ, LANES), lambda b, i: (b * nq + i, 0)),
            pl.BlockSpec((nq, TQ, LANES), lambda b, i: (0, 0, 0)),
            pl.BlockSpec((nq, LANES, TQ), lambda b, i: (0, 0, 0)),
        ],
        out_specs=pl.BlockSpec((TQ, W_C), lambda b, i: (b * nq + i, 0)),
        out_shape=jax.ShapeDtypeStruct((n, W_C), BF16),
        scratch_shapes=[
            pltpu.VMEM((LANES, LANES), BF16), pltpu.VMEM((LANES, LANES), BF16),
            pltpu.VMEM((G_C, nq, TQ, TQ), F32), pltpu.VMEM((n_win, TQ, TQ), F32),
            pltpu.VMEM((H_C, TQ, LANES), F32), pltpu.VMEM((H_C, TQ, LANES), F32),
            pltpu.VMEM((H_C, TQ, LANES), F32), pltpu.VMEM((H_C, TQ, LANES), F32),
            pltpu.VMEM((TQ // LANES, LANES, LANES), F32),
        ],
        compiler_params=_params("arbitrary", "arbitrary"), name="nsa_prompt",
    )(q_all, nsatb, gc, pool, expand)


AUG_POS = 0
AUG_BLK = 16
POS_RADIX = 256


def _t_flash_init(m_scr, l_scr, acc_scr):
    m_scr[...] = jnp.full(m_scr.shape, NEG_INF, F32)
    l_scr[...] = jnp.zeros(l_scr.shape, F32)
    acc_scr[...] = jnp.zeros(acc_scr.shape, F32)


def _t_flash_update(st, idx, m_scr, l_scr, acc_scr, vt):
    m_prev = m_scr[idx]
    m_next = jnp.maximum(m_prev, jnp.max(st, axis=0, keepdims=True))
    alpha = jnp.exp(m_prev - m_next)
    p = jnp.exp(st - m_next)
    l_scr[idx] = alpha * l_scr[idx] + jnp.sum(p, axis=0, keepdims=True)
    m_scr[idx] = m_next
    acc_scr[idx] = alpha * acc_scr[idx] + _dot(vt, p.astype(BF16))


def _causal_bias_t(n):
    k = lax.broadcasted_iota(jnp.int32, (n, n), 0)
    q = lax.broadcasted_iota(jnp.int32, (n, n), 1)
    return jnp.where(k <= q, 0.0, NEG_INF).astype(F32)


def _scale_q(qt):
    return (qt.astype(F32) * QK_SCALE).astype(BF16)


def _alibi_rows(slope, tq, block_bias=None):
    row = lax.broadcasted_iota(jnp.int32, (AUG_BLK, tq), 0)
    head = jnp.where(row == AUG_POS, POS_RADIX * slope, jnp.where(row == AUG_POS + 1, slope, 0.0)).astype(BF16)
    if block_bias is None:
        return jnp.concatenate([head, jnp.zeros((LANES - AUG_BLK, tq), BF16)], axis=0)
    nb = block_bias.shape[0]
    return jnp.concatenate([head, block_bias.astype(BF16), jnp.zeros((LANES - AUG_BLK - nb, tq), BF16)], axis=0)


def _pos_aug(pos, n_blocks=0):
    a = np.zeros((pos.shape[0], LANES), np.float32)
    a[:, AUG_POS] = pos // POS_RADIX
    a[:, AUG_POS + 1] = pos % POS_RADIX
    if n_blocks:
        assert AUG_BLK + n_blocks <= LANES
        a[np.arange(pos.shape[0]), AUG_BLK + pos // SEL_BLOCK] = 1.0
    return a


def _diff_prompt_t_body(lam_ref, gh_ref, qt_ref, k_ref, vt_ref, kaug_ref, o_ref, qa_scr, m_scr, l_scr, acc_scr, *,
                        lam_init):
    i = pl.program_id(1)
    tq = qt_ref.shape[1]
    lam = _diff_lambda(lam_ref, lam_init)
    tri = _causal_bias_t(tq)
    zeros = jnp.zeros((HEAD_DIM, tq), BF16)
    for h in range(H_A):
        qh = _scale_q(qt_ref[h * DA:(h + 1) * DA, :])
        aug = _alibi_rows(_alibi_slope(h, H_A), tq)
        qa_scr[h, :, 0:tq] = jnp.concatenate([qh[0:HEAD_DIM], zeros, aug], axis=0)
        qa_scr[h, :, tq:2 * tq] = jnp.concatenate([zeros, qh[HEAD_DIM:DA], aug], axis=0)
    _t_flash_init(m_scr, l_scr, acc_scr)

    def step(j, extra):
        rows = pl.ds(pl.multiple_of(j * tq, tq), tq)
        kaug = kaug_ref[j]
        for h in range(H_A):
            ka = jnp.concatenate([k_ref[rows, h * DA:(h + 1) * DA], kaug], axis=1)
            st2 = _dot(ka, qa_scr[h])
            vt = vt_ref[j, h * DA:(h + 1) * DA, :]
            for mi in range(2):
                st = st2[:, mi * tq:(mi + 1) * tq]
                if extra is not None:
                    st = st + extra
                _t_flash_update(st, 2 * h + mi, m_scr, l_scr, acc_scr, vt)

    def body(j, c):
        step(j, None)
        return c

    lax.fori_loop(0, i, body, 0)
    step(i, tri)
    for h in range(H_A):
        ot = acc_scr[2 * h] / l_scr[2 * h] - lam * (acc_scr[2 * h + 1] / l_scr[2 * h + 1])
        o_ref[:, h * DA:(h + 1) * DA] = _head_norm(ot.T, gh_ref[h:h + 1, :], lam_init).astype(BF16)


def _diff_prompt_t(lam_vec, g_head, qt, ka, vat, n_batch, seq, lam_init):
    nq = seq // TQ
    n = ka.shape[0]
    kaug = jnp.asarray(_pos_aug(np.arange(seq)).reshape(nq, TQ, LANES), BF16)
    return pl.pallas_call(
        functools.partial(_diff_prompt_t_body, lam_init=lam_init),
        grid=(n_batch, nq),
        in_specs=[
            pl.BlockSpec((4, HEAD_DIM), lambda b, i: (0, 0)),
            pl.BlockSpec((H_A, DA), lambda b, i: (0, 0)),
            pl.BlockSpec((None, None, W_A, TQ), lambda b, i: (b, i, 0, 0)),
            pl.BlockSpec((seq, W_A), lambda b, i: (b, 0)),
            pl.BlockSpec((None, nq, W_A, TQ), lambda b, i: (b, 0, 0, 0)),
            pl.BlockSpec((nq, TQ, LANES), lambda b, i: (0, 0, 0)),
        ],
        out_specs=pl.BlockSpec((TQ, W_A), lambda b, i: (b * nq + i, 0)),
        out_shape=jax.ShapeDtypeStruct((n, W_A), BF16),
        scratch_shapes=[pltpu.VMEM((H_A, 2 * LANES, 2 * TQ), BF16), pltpu.VMEM((2 * H_A, 1, TQ), F32),
                        pltpu.VMEM((2 * H_A, 1, TQ), F32), pltpu.VMEM((2 * H_A, DA, TQ), F32)],
        compiler_params=_params("arbitrary", "arbitrary"), name="diff_prompt",
    )(lam_vec, g_head, qt, ka, vat, kaug)


def _sublane_cumsum(x):
    n = x.shape[0]
    row = lax.broadcasted_iota(jnp.int32, x.shape, 0)
    sh = 1
    while sh < n:
        x = x + jnp.where(row >= sh, pltpu.roll(x, sh, axis=0), 0.0)
        sh *= 2
    return x


def _split3(x):
    hi = x.astype(BF16).astype(F32)
    r = x - hi
    mid = r.astype(BF16).astype(F32)
    lo = (r - mid).astype(BF16).astype(F32)
    return hi, mid, lo


def _fox_prompt_t_body(qt_ref, k_ref, vt_ref, lf_ref, o_ref, daug_scr, qa_scr, m_scr, l_scr, acc_scr, ot_scr):
    i = pl.program_id(1)
    tq = qt_ref.shape[1]
    nchunk = vt_ref.shape[0]

    @pl.when(i == 0)
    def _():
        hi, mid, lo = _split3(_sublane_cumsum(lf_ref[...]))
        lane = lax.broadcasted_iota(jnp.int32, hi.shape, 1)
        d = jnp.where(lane < H_B, hi, 0.0)
        d = d + jnp.where((lane >= H_B) & (lane < 2 * H_B), pltpu.roll(mid, H_B, axis=1), 0.0)
        d = d + jnp.where((lane >= 2 * H_B) & (lane < 3 * H_B), pltpu.roll(lo, 2 * H_B, axis=1), 0.0)
        d = d.astype(BF16)
        for c in range(nchunk):
            daug_scr[c] = d[c * tq:(c + 1) * tq, :]

    tri = _causal_bias_t(tq)
    zeros = jnp.zeros((HEAD_DIM, tq), BF16)
    row = lax.broadcasted_iota(jnp.int32, (LANES, tq), 0)
    for h in range(H_B):
        g = h // (H_B // KV_B)
        pos = PERM_B.index(h)
        qh = _scale_q(qt_ref[pos * HEAD_DIM:(pos + 1) * HEAD_DIM, :])
        aug = jnp.where((row == h) | (row == H_B + h) | (row == 2 * H_B + h), -1.0, 0.0).astype(BF16)
        qa_scr[h] = jnp.concatenate([qh, zeros, aug] if g % 2 == 0 else [zeros, qh, aug], axis=0)
    _t_flash_init(m_scr, l_scr, acc_scr)

    def step(j, extra):
        rows = pl.ds(pl.multiple_of(j * tq, tq), tq)
        daug = daug_scr[j]
        for gp in range(KV_B // 2):
            ka = jnp.concatenate([k_ref[rows, gp * LANES:(gp + 1) * LANES], daug], axis=1)
            for h in range(4 * gp, 4 * gp + 4):
                g = h // (H_B // KV_B)
                st = _dot(ka, qa_scr[h])
                if extra is not None:
                    st = st + extra
                _t_flash_update(st, h, m_scr, l_scr, acc_scr, vt_ref[j, g * HEAD_DIM:(g + 1) * HEAD_DIM, :])

    def body(j, c):
        step(j, None)
        return c

    lax.fori_loop(0, i, body, 0)
    step(i, tri)
    for h in range(H_B):
        pos = PERM_B.index(h)
        ot_scr[pos * HEAD_DIM:(pos + 1) * HEAD_DIM, :] = acc_scr[h] / l_scr[h]
    o_ref[...] = ot_scr[...].T.astype(BF16)


def _fox_prompt_t(qt, kb, vbt, lfn, n_batch, seq):
    nq = seq // TQ
    n = kb.shape[0]
    kvw = KV_B * HEAD_DIM
    return pl.pallas_call(
        _fox_prompt_t_body,
        grid=(n_batch, nq),
        in_specs=[
            pl.BlockSpec((None, None, W_B, TQ), lambda b, i: (b, i, 1, 0)),
            pl.BlockSpec((seq, kvw), lambda b, i: (b, 0)),
            pl.BlockSpec((None, nq, kvw, TQ), lambda b, i: (b, 0, 0, 0)),
            pl.BlockSpec((seq, LANES), lambda b, i: (b, 0)),
        ],
        out_specs=pl.BlockSpec((TQ, W_B), lambda b, i: (b * nq + i, 0)),
        out_shape=jax.ShapeDtypeStruct((n, W_B), BF16),
        scratch_shapes=[pltpu.VMEM((nq, TQ, LANES), BF16), pltpu.VMEM((H_B, 2 * LANES, TQ), BF16),
                        pltpu.VMEM((H_B, 1, TQ), F32), pltpu.VMEM((H_B, 1, TQ), F32),
                        pltpu.VMEM((H_B, HEAD_DIM, TQ), F32), pltpu.VMEM((W_B, TQ), F32)],
        compiler_params=_params("arbitrary", "arbitrary"), name="fox_prompt",
    )(qt, kb, vbt, lfn)


def _select_blocks_t(imp_scr, t, qpos, n_blocks):
    per = SEL_BLOCK // CMP_STRIDE
    bs = imp_scr[t, pl.ds(0, n_blocks, stride=per), :]
    for j in range(1, per):
        bs = bs + imp_scr[t, pl.ds(j, n_blocks, stride=per), :]
    blk = lax.broadcasted_iota(jnp.int32, bs.shape, 0)
    valid = blk * SEL_BLOCK <= qpos
    cur = qpos >> 6
    forced = (blk == 0) | (blk == cur) | (blk == cur - 1)
    score = jnp.where(valid, bs + FORCE_BONUS * jnp.where(forced, 1.0, 0.0), NEG_INF)
    cnt = jnp.zeros(bs.shape, F32)
    for jp in range(n_blocks):
        v = score[jp:jp + 1, :]
        better = (v > score) | ((v == score) & (blk > jp))
        cnt = cnt + jnp.where(better, 1.0, 0.0)
    return jnp.where(valid & (cnt < float(min(N_SEL, n_blocks))), 0.0, NEG_INF)


def _nsa_prompt_t_body(qt_ref, k_ref, vt_ref, gct_ref, pool_ref, poolt_ref, caug_ref, kpos_ref, ksel_ref, o_ref,
                       ck_scr, cvt_scr, imp_scr, qa_scr, m_scr, l_scr, acc_scr, oc_scr, ot_scr):
    i = pl.program_id(1)
    tq = qt_ref.shape[1]
    nchunk = vt_ref.shape[0]
    n_blocks = nchunk * tq // SEL_BLOCK
    n_cmp = LANES
    gw = G_C * HEAD_DIM
    nh = H_C // G_C
    q0 = i * tq
    assert WINDOW % tq == 0 and SEL_BLOCK == 64 and n_blocks * (SEL_BLOCK // CMP_STRIDE) == n_cmp
    wchunks = WINDOW // tq

    @pl.when(i == 0)
    def _():
        ck = jnp.zeros((n_cmp, gw), F32)
        cvt = jnp.zeros((gw, n_cmp), F32)
        for c in range(nchunk):
            ck = ck + _dot(poolt_ref[c], k_ref[c * tq:(c + 1) * tq, 0:gw])
            cvt = cvt + _dot(vt_ref[c, 0:gw, :], pool_ref[c])
        ck_scr[...] = jnp.concatenate([ck.astype(BF16), caug_ref[...]], axis=1)
        cvt_scr[...] = cvt.astype(BF16)

    crow = lax.broadcasted_iota(jnp.int32, (n_cmp, tq), 0)
    qlane = lax.broadcasted_iota(jnp.int32, (n_cmp, tq), 1)
    cmp_ok = crow * CMP_STRIDE + (CMP_BLOCK - 1) <= q0 + qlane
    cmp_bias = jnp.where(cmp_ok, 0.0, NEG_INF)
    qpos_b = q0 + lax.broadcasted_iota(jnp.int32, (n_blocks, tq), 1)
    tri = _causal_bias_t(tq)
    kk = lax.broadcasted_iota(jnp.int32, (tq, tq), 0)
    qq = lax.broadcasted_iota(jnp.int32, (tq, tq), 1)
    win_edge = jnp.where(kk > qq, 0.0, NEG_INF)
    zeros = jnp.zeros((HEAD_DIM, tq), BF16)

    def gate(br, h):
        r = H_B + br * H_C + h
        return jax.nn.sigmoid(gct_ref[r:r + 1, :])

    def put_q(h, block_bias):
        g = h // nh
        pos = PERM_C.index(h)
        qh = _scale_q(qt_ref[pos * HEAD_DIM:(pos + 1) * HEAD_DIM, :])
        aug = _alibi_rows(_alibi_slope(h, H_C), tq, block_bias)
        qa_scr[h] = jnp.concatenate([qh, zeros, aug] if g == 0 else [zeros, qh, aug], axis=0)

    for g in range(G_C):
        imp = jnp.zeros((n_cmp, tq), F32)
        for h in range(nh * g, nh * (g + 1)):
            put_q(h, None)
            st = _dot(ck_scr[...], qa_scr[h]) + cmp_bias
            p = jnp.exp(st - jnp.max(st, axis=0, keepdims=True))
            p = jnp.where(cmp_ok, p / jnp.sum(p, axis=0, keepdims=True), 0.0)
            imp = imp + p
            oc_scr[h] = gate(0, h) * _dot(cvt_scr[g * HEAD_DIM:(g + 1) * HEAD_DIM, :], p.astype(BF16))
        for t in range(tq // LANES):
            imp_scr[t] = imp[:, t * LANES:(t + 1) * LANES]
        bias = jnp.concatenate(
            [_select_blocks_t(imp_scr.at[t], qpos_b[:, t * LANES:(t + 1) * LANES], n_blocks) for t in range(tq // LANES)],
            axis=1)
        for h in range(nh * g, nh * (g + 1)):
            put_q(h, bias)

    def attend(kcol, vrow, kaug, steps_fn, br):
        _t_flash_init(m_scr, l_scr, acc_scr)

        def step(c, extra):
            rows = pl.ds(pl.multiple_of(c * tq, tq), tq)
            ka = jnp.concatenate([k_ref[rows, kcol * gw:(kcol + 1) * gw], kaug[c]], axis=1)
            for h in range(H_C):
                g = h // nh
                st = _dot(ka, qa_scr[h])
                if extra is not None:
                    st = st + extra
                _t_flash_update(st, h, m_scr, l_scr, acc_scr,
                                vt_ref[c, vrow * gw + g * HEAD_DIM:vrow * gw + (g + 1) * HEAD_DIM, :])

        steps_fn(step)
        for h in range(H_C):
            oc_scr[h] = oc_scr[h] + gate(br, h) * (acc_scr[h] / l_scr[h])

    def sel_steps(step):
        def body(c, carry):
            step(c, None)
            return carry
        lax.fori_loop(0, i, body, 0)
        step(i, tri)

    def win_steps(step):
        @pl.when(i >= wchunks)
        def _():
            step(i - wchunks, win_edge)
        for back in range(wchunks - 1, 0, -1):
            @pl.when(i >= back)
            def _(back=back):
                step(i - back, None)
        step(i, tri)

    attend(1, 1, ksel_ref, sel_steps, 1)
    attend(2, 2, kpos_ref, win_steps, 2)
    for h in range(H_C):
        pos = PERM_C.index(h)
        ot_scr[pos * HEAD_DIM:(pos + 1) * HEAD_DIM, :] = oc_scr[h]
    o_ref[...] = ot_scr[...].T.astype(BF16)


def _nsa_prompt_t(qt, kc, vct, fbgct, n_batch, seq):
    nq = seq // TQ
    n = kc.shape[0]
    gw = G_C * HEAD_DIM
    n_blocks = seq // SEL_BLOCK
    pool_np = _pool_matrix(seq, LANES)
    pool = jnp.asarray(pool_np.reshape(nq, TQ, LANES), BF16)
    poolt = jnp.asarray(pool_np.reshape(nq, TQ, LANES).transpose(0, 2, 1), BF16)
    caug = jnp.asarray(_pos_aug(np.arange(LANES) * CMP_STRIDE + CMP_BLOCK - 1), BF16)
    kpos = jnp.asarray(_pos_aug(np.arange(seq)).reshape(nq, TQ, LANES), BF16)
    ksel = jnp.asarray(_pos_aug(np.arange(seq), n_blocks).reshape(nq, TQ, LANES), BF16)
    const3 = lambda b, i: (0, 0, 0)
    return pl.pallas_call(
        _nsa_prompt_t_body,
        grid=(n_batch, nq),
        in_specs=[
            pl.BlockSpec((None, None, W_C, TQ), lambda b, i: (b, i, 2, 0)),
            pl.BlockSpec((seq, 3 * gw), lambda b, i: (b, 0)),
            pl.BlockSpec((None, nq, 3 * gw, TQ), lambda b, i: (b, 0, 0, 0)),
            pl.BlockSpec((None, 4 * H_B, TQ), lambda b, i: (b, 0, i)),
            pl.BlockSpec((nq, TQ, LANES), const3),
            pl.BlockSpec((nq, LANES, TQ), const3),
            pl.BlockSpec((LANES, LANES), lambda b, i: (0, 0)),
            pl.BlockSpec((nq, TQ, LANES), const3),
            pl.BlockSpec((nq, TQ, LANES), const3),
        ],
        out_specs=pl.BlockSpec((TQ, W_C), lambda b, i: (b * nq + i, 0)),
        out_shape=jax.ShapeDtypeStruct((n, W_C), BF16),
        scratch_shapes=[
            pltpu.VMEM((LANES, 2 * LANES), BF16), pltpu.VMEM((gw, LANES), BF16),
            pltpu.VMEM((TQ // LANES, LANES, LANES), F32), pltpu.VMEM((H_C, 2 * LANES, TQ), BF16),
            pltpu.VMEM((H_C, 1, TQ), F32), pltpu.VMEM((H_C, 1, TQ), F32), pltpu.VMEM((H_C, HEAD_DIM, TQ), F32),
            pltpu.VMEM((H_C, HEAD_DIM, TQ), F32), pltpu.VMEM((W_C, TQ), F32),
        ],
        compiler_params=_params("arbitrary", "arbitrary"), name="nsa_prompt",
    )(qt, kc, vct, fbgct, pool, poolt, caug, kpos, ksel)


def _pad_rows(x, n):
    return jnp.concatenate([x, jnp.zeros((n - x.shape[0], x.shape[1]), x.dtype)], axis=0).astype(BF16)


def _query_index(shape, t):
    assert t & (t - 1) == 0
    return lax.broadcasted_iota(jnp.int32, shape, 0) & (t - 1)


def _two_piece_softmax(s_past, s_new):
    m = jnp.maximum(jnp.max(s_past, axis=1, keepdims=True), jnp.max(s_new, axis=1, keepdims=True))
    p_past = jnp.exp(s_past - m)
    p_new = jnp.exp(s_new - m)
    l = jnp.sum(p_past, axis=1, keepdims=True) + jnp.sum(p_new, axis=1, keepdims=True)
    return p_past / l, p_new / l


def _diff_decode_body(pt_ref, lam_ref, gh_ref, q_ref, kvn_ref, *rest, lam_init, n_pages):
    pages = rest[:n_pages]
    o_ref = rest[n_pages]
    t = q_ref.shape[0]
    past = n_pages * PAGE
    lam = _diff_lambda(lam_ref, lam_init)
    ti = _query_index((2 * t, LANES), t)
    lane = lax.broadcasted_iota(jnp.int32, (2 * t, LANES), 1)
    new_bias = jnp.where(lane <= ti, 0.0, NEG_INF)
    kpos_past = (lax.broadcasted_iota(jnp.int32, (1, past), 1) - past).astype(F32)
    kpos_new = lane.astype(F32)
    for h in range(H_A):
        cols = slice(h * DA, (h + 1) * DA)
        q1, q2 = _scaled_halves(q_ref[:, cols])
        qq = jnp.concatenate([q1, q2], axis=0)
        slope = _alibi_slope(h, H_A)
        s_past = jnp.concatenate(
            [_dot_nt(qq, pg[pl.ds(h, PAGE, stride=2 * H_A), :].astype(BF16)) for pg in pages], axis=1)
        s_past = s_past + slope * kpos_past
        k_new = _pad_rows(kvn_ref[:, cols], PAGE)
        v_new = _pad_rows(kvn_ref[:, W_A + h * DA:W_A + (h + 1) * DA], PAGE)
        s_new = _dot_nt(qq, k_new) + slope * kpos_new + new_bias
        p_past, p_new = _two_piece_softmax(s_past, s_new)
        a_past = (p_past[0:t] - lam * p_past[t:2 * t]).astype(BF16)
        a_new = (p_new[0:t] - lam * p_new[t:2 * t]).astype(BF16)
        o = _dot(a_new, v_new)
        for p, pg in enumerate(pages):
            o = o + _dot(a_past[:, p * PAGE:(p + 1) * PAGE], pg[pl.ds(H_A + h, PAGE, stride=2 * H_A), :].astype(BF16))
        o_ref[:, cols] = _head_norm(o, gh_ref[h:h + 1, :], lam_init)


def _page_specs(block, layer, n_pages):
    return [pl.BlockSpec((None, None) + block, functools.partial(lambda p, b, pt: (layer, pt[b, p], 0, 0), p))
            for p in range(n_pages)]


def _diff_decode(page_table, lam_vec, g_head, q_all, dkvb, cache, layer, t_dec, lam_init):
    n_req, n_pages = page_table.shape
    rows = PAGE * 2 * H_A
    grid_spec = pltpu.PrefetchScalarGridSpec(
        num_scalar_prefetch=1, grid=(n_req,),
        in_specs=[
            pl.BlockSpec((4, HEAD_DIM), lambda b, pt: (0, 0)),
            pl.BlockSpec((H_A, DA), lambda b, pt: (0, 0)),
            pl.BlockSpec((t_dec, W_A), lambda b, pt: (b, 0)),
            pl.BlockSpec((t_dec, 2 * W_A), lambda b, pt: (b, 0)),
        ] + _page_specs((rows, DA), layer, n_pages),
        out_specs=pl.BlockSpec((t_dec, W_A), lambda b, pt: (b, 0)),
    )
    return pl.pallas_call(
        functools.partial(_diff_decode_body, lam_init=lam_init, n_pages=n_pages),
        grid_spec=grid_spec, out_shape=jax.ShapeDtypeStruct((n_req * t_dec, W_A), F32),
        compiler_params=_params("arbitrary"), name="diff_decode",
    )(page_table, lam_vec, g_head, q_all, dkvb, *([cache] * n_pages))


def _fox_decode_body(pt_ref, q_ref, kvn_ref, lfn_ref, *rest, n_pages):
    pages = rest[:n_pages]
    lf_pages = rest[n_pages:2 * n_pages]
    o_ref = rest[2 * n_pages]
    t = q_ref.shape[0]
    past = n_pages * PAGE
    nh = 4
    d_all = _lane_cumsum(jnp.concatenate([lf[...] for lf in lf_pages] + [lfn_ref[...]], axis=1))
    ti = _query_index((nh * t, LANES), t)
    lane = lax.broadcasted_iota(jnp.int32, (nh * t, LANES), 1)
    new_bias = jnp.where(lane <= ti, 0.0, NEG_INF)
    lo, _ = _half_masks((t, LANES))
    for gp in range(KV_B // 2):
        heads = _pair_heads_b(gp)
        halves = [_scaled_halves(q_ref[:, tl * LANES:(tl + 1) * LANES]) for tl in (2 * gp, 2 * gp + 1)]
        qq = jnp.concatenate([halves[tl - 2 * gp][half] for tl, half, _ in heads], axis=0)
        d_rows = jnp.concatenate([jnp.broadcast_to(d_all[h:h + 1, :], (t, past + LANES)) for _, _, h in heads], axis=0)
        krows = slice(gp * LANES, (gp + 1) * LANES)
        vrows = slice(KV_B * HEAD_DIM + gp * LANES, KV_B * HEAD_DIM + (gp + 1) * LANES)
        s_past = jnp.concatenate([_dot(qq, pg[krows, :].astype(BF16)) for pg in pages], axis=1) - d_rows[:, :past]
        k_new = _pad_rows(kvn_ref[:, krows], PAGE)
        v_new = _pad_rows(kvn_ref[:, vrows], PAGE)
        s_new = _dot_nt(qq, k_new) - d_rows[:, past:] + new_bias
        p_past, p_new = _two_piece_softmax(s_past, s_new)
        p_past = p_past.astype(BF16)
        o = _dot(p_new.astype(BF16), v_new)
        for p, pg in enumerate(pages):
            o = o + _dot_nt(p_past[:, p * PAGE:(p + 1) * PAGE], pg[vrows, :].astype(BF16))
        o_ref[:, (2 * gp) * LANES:(2 * gp + 1) * LANES] = jnp.where(lo, o[0:t], o[t:2 * t])
        o_ref[:, (2 * gp + 1) * LANES:(2 * gp + 2) * LANES] = jnp.where(lo, o[2 * t:3 * t], o[3 * t:4 * t])


def _fox_decode(page_table, q_all, foxb, lfn, cache_kv, cache_lf, layer, t_dec):
    n_req, n_pages = page_table.shape
    grid_spec = pltpu.PrefetchScalarGridSpec(
        num_scalar_prefetch=1, grid=(n_req,),
        in_specs=[
            pl.BlockSpec((t_dec, W_B), lambda b, pt: (b, 1)),
            pl.BlockSpec((t_dec, 2 * KV_B * HEAD_DIM), lambda b, pt: (b, 0)),
            pl.BlockSpec((None, H_B, LANES), lambda b, pt: (b, 0, 0)),
        ] + _page_specs((2 * KV_B * HEAD_DIM, PAGE), layer, n_pages) + _page_specs((H_B, PAGE), layer, n_pages),
        out_specs=pl.BlockSpec((t_dec, W_B), lambda b, pt: (b, 0)),
    )
    return pl.pallas_call(
        functools.partial(_fox_decode_body, n_pages=n_pages),
        grid_spec=grid_spec, out_shape=jax.ShapeDtypeStruct((n_req * t_dec, W_B), F32),
        compiler_params=_params("arbitrary"), name="fox_decode",
    )(page_table, q_all, foxb, lfn, *([cache_kv] * n_pages), *([cache_lf] * n_pages))


def _nsa_decode_body(pt_ref, q_ref, kvn_ref, gc_ref, win_ref, pool_ref, *rest, n_pages):
    pages = rest[:n_pages]
    o_ref = rest[n_pages]
    t = q_ref.shape[0]
    past = n_pages * PAGE
    wbuf = win_ref.shape[1]
    nh = H_C // G_C
    n_blocks = (past + SEL_BLOCK) // SEL_BLOCK
    n_sel_lanes = 2 * LANES
    ti = _query_index((nh * t, LANES), t)
    lane = lax.broadcasted_iota(jnp.int32, (nh * t, LANES), 1)
    tq = past + ti

    ck = jnp.zeros((LANES, LANES), F32)
    cv = jnp.zeros((LANES, LANES), F32)
    for p, pg in enumerate(pages):
        ck = ck + _dot(pg[0:LANES, :].astype(BF16), pool_ref[p])
        cv = cv + _dot(pg[LANES:2 * LANES, :].astype(BF16), pool_ref[p])
    ckt = ck.astype(BF16)
    cvt = cv.astype(BF16)
    cpos = lane * CMP_STRIDE + (CMP_BLOCK - 1)
    cmp_ok = cpos <= tq
    cmp_bias = jnp.where(cmp_ok, 0.0, NEG_INF)
    cpos_rel = (cpos - past).astype(F32)

    kpos_past = lax.broadcasted_iota(jnp.int32, (1, past), 1)
    kpos_past_rel = (kpos_past - past).astype(F32)
    new_ok = lane <= ti
    kpos_new_rel = lane.astype(F32)
    wpos = past - wbuf + lax.broadcasted_iota(jnp.int32, (nh * t, wbuf), 1)
    tq_w = past + _query_index((nh * t, wbuf), t)
    win_bias = jnp.where((wpos <= tq_w) & (tq_w - wpos < WINDOW) & (wpos >= 0), 0.0, NEG_INF)
    wpos_rel = (wpos - past).astype(F32)
    win_new_bias = jnp.where(new_ok & (ti - lane < WINDOW), 0.0, NEG_INF)

    def gate_col(br, g):
        cols = [jax.nn.sigmoid(gc_ref[:, GC_LANE0 + br * H_C + nh * g + r:GC_LANE0 + br * H_C + nh * g + r + 1])
                for r in range(nh)]
        return jnp.concatenate(cols, axis=0)

    lo, hi = _half_masks((t, LANES))
    outs = []
    for g in range(G_C):
        qq = jnp.concatenate([_scaled_halves(q_ref[:, r * LANES:(r + 1) * LANES])[g] for r in range(nh)], axis=0)
        head_row = lax.broadcasted_iota(jnp.int32, (nh * t, 1), 0)
        slope = jnp.full((nh * t, 1), _alibi_slope(nh * g, H_C), F32)
        for r in range(1, nh):
            slope = jnp.where(head_row >= r * t, _alibi_slope(nh * g + r, H_C), slope)

        s = _dot(qq, ckt) + slope * cpos_rel + cmp_bias
        p = jnp.exp(s - jnp.max(s, axis=1, keepdims=True))
        p = jnp.where(cmp_ok, p / jnp.sum(p, axis=1, keepdims=True), 0.0)
        o = gate_col(0, g) * _dot_nt(p.astype(BF16), cvt)
        imp = p[0:t]
        for r in range(1, nh):
            imp = imp + p[r * t:(r + 1) * t]
        imp = jnp.concatenate([imp, jnp.zeros((t, n_sel_lanes - LANES), F32)], axis=1)

        qpos = past + lax.broadcasted_iota(jnp.int32, (t, n_sel_lanes), 0)
        score, valid, blk = _block_scores(imp, qpos, n_blocks)
        sel = jnp.where(_select_blocks(score, valid, blk, n_blocks), 1.0, 0.0)
        lo_t, _ = _half_masks((t, LANES))
        tiles = []
        for c in range(n_pages + 1):
            a = sel[:, 4 * (2 * c) + 3:4 * (2 * c) + 4]
            b = sel[:, 4 * (2 * c + 1) + 3:4 * (2 * c + 1) + 4] if 2 * c + 1 < n_blocks else jnp.zeros((t, 1), F32)
            tiles.append(jnp.where(lo_t, a, b))
        sel_keys = jnp.concatenate(tiles, axis=1)
        sel_keys = jnp.concatenate([sel_keys] * nh, axis=0)
        sel_bias_past = jnp.where(sel_keys[:, :past] > 0.5, 0.0, NEG_INF)
        sel_bias_new = jnp.where((sel_keys[:, past:] > 0.5) & new_ok, 0.0, NEG_INF)

        s_past = jnp.concatenate([_dot(qq, pg[2 * LANES:3 * LANES, :].astype(BF16)) for pg in pages], axis=1)
        s_past = s_past + slope * kpos_past_rel + sel_bias_past
        k_new = _pad_rows(kvn_ref[:, 2 * LANES:3 * LANES], PAGE)
        v_new = _pad_rows(kvn_ref[:, 3 * LANES:4 * LANES], PAGE)
        s_new = _dot_nt(qq, k_new) + slope * kpos_new_rel + sel_bias_new
        p_past, p_new = _two_piece_softmax(s_past, s_new)
        p_past = p_past.astype(BF16)
        osel = _dot(p_new.astype(BF16), v_new)
        for pi, pg in enumerate(pages):
            osel = osel + _dot_nt(p_past[:, pi * PAGE:(pi + 1) * PAGE], pg[3 * LANES:4 * LANES, :].astype(BF16))
        o = o + gate_col(1, g) * osel

        s_past = _dot(qq, win_ref[0:LANES, :].astype(BF16)) + slope * wpos_rel + win_bias
        k_new = _pad_rows(kvn_ref[:, 4 * LANES:5 * LANES], PAGE)
        v_new = _pad_rows(kvn_ref[:, 5 * LANES:6 * LANES], PAGE)
        s_new = _dot_nt(qq, k_new) + slope * kpos_new_rel + win_new_bias
        p_past, p_new = _two_piece_softmax(s_past, s_new)
        owin = _dot(p_new.astype(BF16), v_new) + _dot_nt(p_past.astype(BF16), win_ref[LANES:2 * LANES, :].astype(BF16))
        o = o + gate_col(2, g) * owin
        outs.append(o)

    for r in range(nh):
        o_ref[:, r * LANES:(r + 1) * LANES] = jnp.where(lo, outs[0][r * t:(r + 1) * t], outs[1][r * t:(r + 1) * t])


def _nsa_decode(page_table, q_all, nsab, fbgc, win_state, cache, layer, t_dec):
    n_req, n_pages = page_table.shape
    past = n_pages * PAGE
    wbuf = win_state.shape[-1]
    pool = jnp.asarray(_pool_matrix(past, LANES).reshape(n_pages, PAGE, LANES), BF16)
    grid_spec = pltpu.PrefetchScalarGridSpec(
        num_scalar_prefetch=1, grid=(n_req,),
        in_specs=[
            pl.BlockSpec((t_dec, W_C), lambda b, pt: (b, 2)),
            pl.BlockSpec((t_dec, 6 * G_C * HEAD_DIM), lambda b, pt: (b, 0)),
            pl.BlockSpec((t_dec, LANES), lambda b, pt: (b, 0)),
            pl.BlockSpec((None, None, 2 * G_C * HEAD_DIM, wbuf), lambda b, pt: (layer, b, 0, 0)),
            pl.BlockSpec((n_pages, PAGE, LANES), lambda b, pt: (0, 0, 0)),
        ] + _page_specs((4 * G_C * HEAD_DIM, PAGE), layer, n_pages),
        out_specs=pl.BlockSpec((t_dec, W_C), lambda b, pt: (b, 0)),
    )
    return pl.pallas_call(
        functools.partial(_nsa_decode_body, n_pages=n_pages),
        grid_spec=grid_spec, out_shape=jax.ShapeDtypeStruct((n_req * t_dec, W_C), F32),
        compiler_params=_params("arbitrary"), name="nsa_decode",
    )(page_table, q_all, nsab, fbgc, win_state, pool, *([cache] * n_pages))


def _finish_body(x_ref, oa_ref, ob_ref, oc_ref, gate_ref, wa_ref, wb_ref, wc_ref, wo_ref, g_ref, y_ref):
    m = gate_ref[:, 0:D_MODEL].astype(F32) * _dot(oa_ref[...].astype(BF16), wa_ref[...])
    m = m + gate_ref[:, D_MODEL:2 * D_MODEL].astype(F32) * _dot(ob_ref[...].astype(BF16), wb_ref[...])
    m = m + gate_ref[:, 2 * D_MODEL:3 * D_MODEL].astype(F32) * _dot(oc_ref[...].astype(BF16), wc_ref[...])
    y = _dot(m.astype(BF16), wo_ref[...])
    y_ref[...] = x_ref[...] + _rms(y, g_ref[...])


def _finish(x, oa, ob, oc, gate, wa, wb, wc, wo, g):
    n = x.shape[0]
    tm = min(TQ, n)
    row = lambda i: (i, 0)
    const = lambda i: (0, 0)
    return pl.pallas_call(
        _finish_body, grid=(n // tm,),
        in_specs=[
            pl.BlockSpec((tm, D_MODEL), row), pl.BlockSpec((tm, W_A), row), pl.BlockSpec((tm, W_B), row),
            pl.BlockSpec((tm, W_C), row), pl.BlockSpec((tm, 3 * D_MODEL), row),
            _resident((W_A, D_MODEL), const), _resident((W_B, D_MODEL), const), _resident((W_C, D_MODEL), const),
            _resident((D_MODEL, D_MODEL), const), pl.BlockSpec((1, D_MODEL), const),
        ],
        out_specs=pl.BlockSpec((tm, D_MODEL), row),
        out_shape=jax.ShapeDtypeStruct((n, D_MODEL), F32),
        compiler_params=_params("arbitrary"), name="finish",
    )(x, oa, ob, oc, gate, wa, wb, wc, wo, g)


def _mlp_body(x_ref, g2_ref, g3_ref, wu_ref, wd_ref, y_ref):
    x = x_ref[...]
    h = _rms(x, g2_ref[...]).astype(BF16)
    fc = D_MODEL
    d = jnp.zeros(x.shape, F32)
    for c in range(D_FF // fc):
        u = jnp.maximum(_dot(h, wu_ref[:, c * fc:(c + 1) * fc]), 0.0)
        d = d + _dot((u * u).astype(BF16), wd_ref[c * fc:(c + 1) * fc, :])
    y_ref[...] = x + _rms(d, g3_ref[...])


def _mlp(x, g2, g3, wu, wd):
    n = x.shape[0]
    tm = min(TQ, n)
    row = lambda i: (i, 0)
    const = lambda i: (0, 0)
    return pl.pallas_call(
        _mlp_body, grid=(n // tm,),
        in_specs=[
            pl.BlockSpec((tm, D_MODEL), row), pl.BlockSpec((1, D_MODEL), const), pl.BlockSpec((1, D_MODEL), const),
            _resident((D_MODEL, D_FF), const), _resident((D_FF, D_MODEL), const),
        ],
        out_specs=pl.BlockSpec((tm, D_MODEL), row),
        out_shape=jax.ShapeDtypeStruct((n, D_MODEL), F32),
        compiler_params=_params("arbitrary"), name="mlp",
    )(x, g2, g3, wu, wd)


def _prep_w_in(w_in):
    wt = jnp.transpose(w_in, (0, 2, 1))

    def rows(lo, n):
        return wt[:, lo:lo + n, :].astype(BF16)

    pieces = [rows(_OFF_KA, 2 * W_A), rows(_OFF_KB, 2 * KV_B * HEAD_DIM), rows(_OFF_KVC, 6 * G_C * HEAD_DIM),
              rows(_OFF_FB, H_B), rows(_OFF_GC, 3 * H_C),
              jnp.zeros((wt.shape[0], LANES - H_B - 3 * H_C, D_MODEL), BF16),
              rows(_OFF_QA, W_A)]
    pieces += [rows(_OFF_QB + h * HEAD_DIM, HEAD_DIM) for h in PERM_B]
    pieces += [rows(_OFF_QC + h * HEAD_DIM, HEAD_DIM) for h in PERM_C]
    pieces.append(rows(_OFF_GM, 3 * D_MODEL))
    out = jnp.concatenate(pieces, axis=1)
    assert out.shape[1] == NW
    return out


def _perm_rows(w, perm):
    return jnp.concatenate([w[:, h * HEAD_DIM:(h + 1) * HEAD_DIM, :] for h in perm], axis=1).astype(BF16)


def kernel(x_prompt, x_sample, cache_diff_kv, cache_fox_kv, cache_fox_logf, cache_nsa_kv, state_nsa_win_kv, page_table, w_in, b_f, diff_lam, diff_norm_g, w_branch_a, w_branch_b, w_branch_c, w_out, norm_g, w_up, w_down):
    n_batch, seq, _ = x_prompt.shape
    n_req, t_dec, _ = x_sample.shape
    depth = w_in.shape[0]
    n_phys = cache_diff_kv.shape[1]
    n_pages = page_table.shape[1]
    assert seq % TQ == 0

    wt_all = _prep_w_in(w_in)
    wa_all = w_branch_a.astype(BF16)
    wb_all = _perm_rows(w_branch_b, PERM_B)
    wc_all = _perm_rows(w_branch_c, PERM_C)
    wo_all = w_out.astype(BF16)
    wu_all = w_up.astype(BF16)
    wd_all = w_down.astype(BF16)

    c_diff = cache_diff_kv.reshape(depth, n_phys, PAGE * 2 * H_A, DA)
    c_fox = jnp.transpose(cache_fox_kv, (0, 1, 3, 4, 5, 2)).reshape(depth, n_phys, 2 * KV_B * HEAD_DIM, PAGE)
    c_lf = jnp.transpose(cache_fox_logf, (0, 1, 3, 2))
    c_nsa = jnp.transpose(cache_nsa_kv, (0, 1, 3, 4, 5, 2)).reshape(depth, n_phys, 4 * G_C * HEAD_DIM, PAGE)
    wbuf = state_nsa_win_kv.shape[2]
    c_win = jnp.transpose(state_nsa_win_kv, (0, 1, 3, 4, 5, 2)).reshape(depth, n_req, 2 * G_C * HEAD_DIM, wbuf)

    xp = x_prompt.reshape(n_batch * seq, D_MODEL)
    xs = x_sample.reshape(n_req * t_dec, D_MODEL)
    outs_p = [[] for _ in range(5)]
    outs_s = [[] for _ in range(5)]
    n_win = min(WINDOW, seq)
    for l in range(depth):
        lam_init = 0.8 - 0.6 * float(np.exp(-0.3 * l))
        g = norm_g[l]
        bf_col = b_f[l].reshape(H_B, 1)
        bf_row = jnp.pad(b_f[l], (0, LANES - H_B)).reshape(1, LANES)
        wt = wt_all[l]

        dkv, ka, vat, qat, foxt, foxtb, nsat, wint, nsatb, logft, gc, q_bc, gate = _inproj_prompt(
            xp, g[0:1], bf_col, wt, n_batch, seq)
        oa = _diff_prompt_t(diff_lam[l], diff_norm_g[l], qat, ka, vat, n_batch, seq, lam_init)
        ob = _fox_prompt(q_bc, foxtb, logft, n_batch, seq)
        oc = _nsa_prompt(q_bc, nsatb, gc, n_batch, seq)
        xp = _finish(xp, oa, ob, oc, gate, wa_all[l], wb_all[l], wc_all[l], wo_all[l], g[1:2])
        xp = _mlp(xp, g[2:3], g[3:4], wu_all[l], wd_all[l])
        for lst, s in zip(outs_p, (dkv, foxt, logft, nsat, wint[:, :, seq - n_win:])):
            lst.append(s)

        dkv, fox, nsa, logft_s, fbgc, q_all, gate = _inproj_sample(xs, g[0:1], bf_col, bf_row, wt)
        lfn = jnp.pad(jnp.transpose(logft_s.reshape(H_B, n_req, t_dec), (1, 0, 2)), ((0, 0), (0, 0), (0, LANES - t_dec)))
        oa = _diff_decode(page_table, diff_lam[l], diff_norm_g[l], q_all, dkv, c_diff, l, t_dec, lam_init)
        ob = _fox_decode(page_table, q_all, fox, lfn, c_fox, c_lf, l, t_dec)
        oc = _nsa_decode(page_table, q_all, nsa, fbgc, c_win, c_nsa, l, t_dec)
        xs = _finish(xs, oa, ob, oc, gate, wa_all[l], wb_all[l], wc_all[l], wo_all[l], g[1:2])
        xs = _mlp(xs, g[2:3], g[3:4], wu_all[l], wd_all[l])
        for lst, s in zip(outs_s, (dkv, fox, fbgc[:, 0:H_B], nsa[:, 0:4 * G_C * HEAD_DIM], nsa[:, 4 * G_C * HEAD_DIM:])):
            lst.append(s)

    def tr(stack, dims):
        a = jnp.stack(stack)
        a = a.reshape(a.shape[:2] + dims + a.shape[3:])
        return jnp.moveaxis(a, -1, 2)

    dkv_p = jnp.stack(outs_p[0]).reshape(depth, n_batch, seq, 2, H_A, DA)
    fkv_p = tr(outs_p[1], (2, KV_B, HEAD_DIM))
    flf_p = tr(outs_p[2], (H_B,))
    nkv_p = tr(outs_p[3], (4, G_C, HEAD_DIM))
    nwin_p = tr(outs_p[4], (2, G_C, HEAD_DIM))
    dkv_s = jnp.stack(outs_s[0]).reshape(depth, n_req, t_dec, 2, H_A, DA)
    fkv_s = jnp.stack(outs_s[1]).reshape(depth, n_req, t_dec, 2, KV_B, HEAD_DIM)
    flf_s = jnp.stack(outs_s[2]).reshape(depth, n_req, t_dec, H_B)
    nkv_s = jnp.stack(outs_s[3]).reshape(depth, n_req, t_dec, 4, G_C, HEAD_DIM)
    nwin_s = jnp.stack(outs_s[4]).reshape(depth, n_req, t_dec, 2, G_C, HEAD_DIM)
    yp = xp.reshape(n_batch, seq, D_MODEL)
    ys = xs.reshape(n_req, t_dec, D_MODEL)
    return (yp, ys, dkv_p, dkv_s, fkv_p, fkv_s, flf_p, flf_s, nkv_p, nkv_s, nwin_p, nwin_s)
```
